```python
import jax, jax.numpy as jnp
from jax import lax
import numpy as np

D_MODEL = 1024
BATCH = 4
SEQ = 8192
DEPTH = 2
DEC_BATCH = 4
DEC_SEQ = 4096
PAST_LEN = 128

FOURIER_WIDTH = 256
FOURIER_GROUPS = 4
FOURIER_GROUP_DIM = FOURIER_WIDTH // FOURIER_GROUPS
N_HEADS = 6
QK_NOPE_DIM = 128
QK_ROPE_DIM = 64
V_HEAD_DIM = 128
Q_LORA_RANK = 384
KV_LORA_RANK = 256
ATTN_WIDTH = N_HEADS * V_HEAD_DIM
MIX_WIDTH = FOURIER_WIDTH + ATTN_WIDTH
IN_WIDTH = FOURIER_WIDTH + Q_LORA_RANK + KV_LORA_RANK + QK_ROPE_DIM
ROPE_THETA = 10000.0
Q_BLOCK = 128
N_EXPERTS = 16
N_GROUPS = 4
EXPERTS_PER_GROUP = N_EXPERTS // N_GROUPS
TOP_K = 2
D_FF_EXPERT = 512
D_FF_SHARED = 512
DEEPNORM_ALPHA = (2 * DEPTH) ** 0.25
DEEPNORM_BETA = (8 * DEPTH) ** -0.25
LN_EPS = 1e-5
RMS_EPS = 1e-6

kernel_name = "hymba_fnet_mla_grouped_moe_deepnorm_encoder"


def _layernorm(x, g, b):
    xf = x.astype(jnp.float32)
    mu = jnp.mean(xf, axis=-1, keepdims=True)
    var = jnp.mean(jnp.square(xf - mu), axis=-1, keepdims=True)
    y = (xf - mu) * lax.rsqrt(var + LN_EPS)
    return (y * g.astype(jnp.float32) + b.astype(jnp.float32)).astype(x.dtype)


def _rmsnorm(x, g):
    xf = x.astype(jnp.float32)
    y = xf * lax.rsqrt(jnp.mean(jnp.square(xf), axis=-1, keepdims=True) + RMS_EPS)
    return (y * g.astype(jnp.float32)).astype(x.dtype)


def _rope_tables(seq_len):
    pos = jnp.arange(seq_len, dtype=jnp.float32)
    inv_freq = ROPE_THETA ** (-jnp.arange(0, QK_ROPE_DIM, 2, dtype=jnp.float32) / QK_ROPE_DIM)
    ang = pos[:, None] * inv_freq[None, :]
    return jnp.cos(ang), jnp.sin(ang)


def _rope(x, cos, sin):
    cos = cos.astype(x.dtype)
    sin = sin.astype(x.dtype)
    x1, x2 = jnp.split(x, 2, axis=-1)
    return jnp.concatenate([x1 * cos - x2 * sin, x2 * cos + x1 * sin], axis=-1)


def _block_attention(q_nope, q_rope, k_nope, k_rope, v):
    B, S, H, _ = q_nope.shape
    nb = S // Q_BLOCK
    scale = (QK_NOPE_DIM + QK_ROPE_DIM) ** -0.5

    def blocks(t):
        return jnp.moveaxis(t.reshape((B, nb, Q_BLOCK) + t.shape[2:]), 1, 0)

    def one_block(qs):
        qn, qr = qs
        s = jnp.einsum('bqhd,bkhd->bhqk', qn, k_nope) + jnp.einsum('bqhr,bkr->bhqk', qr, k_rope)
        p = jax.nn.softmax(s.astype(jnp.float32) * scale, axis=-1).astype(v.dtype)
        return jnp.einsum('bhqk,bkhd->bqhd', p, v)

    o = lax.map(one_block, (blocks(q_nope), blocks(q_rope)))
    return jnp.moveaxis(o, 0, 1).reshape(B, S, H * V_HEAD_DIM)


def _swiglu(t, w_gate, w_up, w_down):
    return (jax.nn.silu(t @ w_gate) * (t @ w_up)) @ w_down


def _moe(x, w_router, router_bias, w_gate, w_up, w_down, ws_gate, ws_up, ws_down):
    B, S, D = x.shape
    t = x.reshape(B * S, D)
    scores = jax.nn.sigmoid((t @ w_router).astype(jnp.float32))
    sel = scores + router_bias.astype(jnp.float32)
    grp = lax.top_k(sel.reshape(-1, N_GROUPS, EXPERTS_PER_GROUP), TOP_K)[0].sum(-1)
    best = jnp.argmax(grp, axis=-1)
    in_group = (jnp.arange(N_EXPERTS) // EXPERTS_PER_GROUP)[None, :] == best[:, None]
    _, idx = lax.top_k(jnp.where(in_group, sel, -jnp.inf), TOP_K)
    w = jnp.take_along_axis(scores, idx, axis=-1)
    w = w / jnp.sum(w, axis=-1, keepdims=True)
    gates = jnp.einsum('tk,tke->te', w, jax.nn.one_hot(idx, N_EXPERTS, dtype=jnp.float32)).astype(x.dtype)
    y = _swiglu(t, ws_gate, ws_up, ws_down)
    for e in range(N_EXPERTS):
        y = y + gates[:, e:e + 1] * _swiglu(t, w_gate[e], w_up[e], w_down[e])
    return y.reshape(B, S, D)


def _layer(x, w_in, q_norm_g, w_q_up, kv_norm_g, w_kv_up, w_fourier, w_out, ln1_g, ln1_b,
           w_router, router_bias, w_gate, w_up, w_down, ws_gate, ws_up, ws_down, ln2_g, ln2_b):
    B, S, _ = x.shape
    h = x @ w_in
    u_f, c_q, c_kv, k_r = jnp.split(
        h, [FOURIER_WIDTH, FOURIER_WIDTH + Q_LORA_RANK, FOURIER_WIDTH + Q_LORA_RANK + KV_LORA_RANK], axis=-1)
    u = u_f.reshape(B, S, FOURIER_GROUPS, FOURIER_GROUP_DIM).astype(jnp.float32)
    f = jnp.real(jnp.fft.fft2(u, axes=(1, 3), norm='ortho')).astype(x.dtype)
    f = jnp.einsum('bsgc,gcd->bsgd', f, w_fourier).reshape(B, S, FOURIER_WIDTH)
    q = (_rmsnorm(c_q, q_norm_g) @ w_q_up).reshape(B, S, N_HEADS, QK_NOPE_DIM + QK_ROPE_DIM)
    q_nope, q_rope = jnp.split(q, [QK_NOPE_DIM], axis=-1)
    kv = (_rmsnorm(c_kv, kv_norm_g) @ w_kv_up).reshape(B, S, N_HEADS, QK_NOPE_DIM + V_HEAD_DIM)
    k_nope, v = jnp.split(kv, [QK_NOPE_DIM], axis=-1)
    cos, sin = _rope_tables(S)
    q_rope = _rope(q_rope, cos[:, None, :], sin[:, None, :])
    k_rope = _rope(k_r, cos, sin)
    a = _block_attention(q_nope, q_rope, k_nope, k_rope, v)
    mix = jnp.concatenate([f, a], axis=-1) @ w_out
    x = _layernorm(DEEPNORM_ALPHA * x + mix, ln1_g, ln1_b)
    m = _moe(x, w_router, router_bias, w_gate, w_up, w_down, ws_gate, ws_up, ws_down)
    return _layernorm(DEEPNORM_ALPHA * x + m, ln2_g, ln2_b)


def _trunk(x, w_in, q_norm_g, w_q_up, kv_norm_g, w_kv_up, w_fourier, w_out, ln1_g, ln1_b,
           w_router, router_bias, w_gate, w_up, w_down, ws_gate, ws_up, ws_down, ln2_g, ln2_b):
    for i in range(DEPTH):
        x = _layer(x, w_in[i], q_norm_g[i], w_q_up[i], kv_norm_g[i], w_kv_up[i], w_fourier[i], w_out[i],
                   ln1_g[i], ln1_b[i], w_router, router_bias, w_gate[i], w_up[i], w_down[i],
                   ws_gate[i], ws_up[i], ws_down[i], ln2_g[i], ln2_b[i])
    return x


def setup_inputs(seed: int = 0) -> dict:
    key = jax.random.key(seed)
    ks = jax.random.split(key, 24)
    f32 = jnp.float32

    def nrm(k, shape, fan_in, scale=1.0):
        return jax.random.normal(k, shape, f32) * (scale * fan_in ** -0.5)

    def gain(k, shape):
        return 1.0 + 0.02 * jax.random.normal(k, shape, f32)

    return {
        "x_prompt": jax.random.normal(ks[0], (BATCH, SEQ, D_MODEL), f32),
        "x_sample": jax.random.normal(ks[1], (DEC_BATCH, DEC_SEQ, D_MODEL), f32),
        "w_in": nrm(ks[2], (DEPTH, D_MODEL, IN_WIDTH), D_MODEL),
        "q_norm_g": gain(ks[3], (DEPTH, Q_LORA_RANK)),
        "w_q_up": nrm(ks[4], (DEPTH, Q_LORA_RANK, N_HEADS * (QK_NOPE_DIM + QK_ROPE_DIM)), Q_LORA_RANK),
        "kv_norm_g": gain(ks[5], (DEPTH, KV_LORA_RANK)),
        "w_kv_up": nrm(ks[6], (DEPTH, KV_LORA_RANK, N_HEADS * (QK_NOPE_DIM + V_HEAD_DIM)), KV_LORA_RANK),
        "w_fourier": nrm(ks[7], (DEPTH, FOURIER_GROUPS, FOURIER_GROUP_DIM, FOURIER_GROUP_DIM), FOURIER_GROUP_DIM),
        "w_out": nrm(ks[8], (DEPTH, MIX_WIDTH, D_MODEL), MIX_WIDTH, DEEPNORM_BETA),
        "ln1_g": gain(ks[9], (DEPTH, D_MODEL)),
        "ln1_b": 0.02 * jax.random.normal(ks[10], (DEPTH, D_MODEL), f32),
        "w_router": nrm(ks[11], (D_MODEL, N_EXPERTS), D_MODEL),
        "router_bias": 0.01 * jax.random.normal(ks[12], (N_EXPERTS,), f32),
        "w_gate": nrm(ks[13], (DEPTH, N_EXPERTS, D_MODEL, D_FF_EXPERT), D_MODEL),
        "w_up": nrm(ks[14], (DEPTH, N_EXPERTS, D_MODEL, D_FF_EXPERT), D_MODEL),
        "w_down": nrm(ks[15], (DEPTH, N_EXPERTS, D_FF_EXPERT, D_MODEL), D_FF_EXPERT, DEEPNORM_BETA),
        "ws_gate": nrm(ks[16], (DEPTH, D_MODEL, D_FF_SHARED), D_MODEL),
        "ws_up": nrm(ks[17], (DEPTH, D_MODEL, D_FF_SHARED), D_MODEL),
        "ws_down": nrm(ks[18], (DEPTH, D_FF_SHARED, D_MODEL), D_FF_SHARED, DEEPNORM_BETA),
        "ln2_g": gain(ks[19], (DEPTH, D_MODEL)),
        "ln2_b": 0.02 * jax.random.normal(ks[20], (DEPTH, D_MODEL), f32),
    }


def reference(x_prompt, x_sample, w_in, q_norm_g, w_q_up, kv_norm_g, w_kv_up, w_fourier, w_out,
              ln1_g, ln1_b, w_router, router_bias, w_gate, w_up, w_down, ws_gate, ws_up, ws_down,
              ln2_g, ln2_b):
    y_prompt = _trunk(x_prompt, w_in, q_norm_g, w_q_up, kv_norm_g, w_kv_up, w_fourier, w_out, ln1_g, ln1_b,
                      w_router, router_bias, w_gate, w_up, w_down, ws_gate, ws_up, ws_down, ln2_g, ln2_b)
    y_sample = _trunk(x_sample, w_in, q_norm_g, w_q_up, kv_norm_g, w_kv_up, w_fourier, w_out, ln1_g, ln1_b,
                      w_router, router_bias, w_gate, w_up, w_down, ws_gate, ws_up, ws_down, ln2_g, ln2_b)
    return (y_prompt, y_sample)
```

```python
import functools
import math

import numpy as np
import jax
import jax.numpy as jnp
from jax import lax
from jax.experimental import pallas as pl
from jax.experimental.pallas import tpu as pltpu

F32 = jnp.float32
BF16 = jnp.bfloat16

D_MODEL = 1024
DEPTH = 2
FOURIER_WIDTH = 256
FOURIER_GROUPS = 4
FOURIER_GROUP_DIM = 64
N_HEADS = 6
QK_NOPE_DIM = 128
QK_ROPE_DIM = 64
V_HEAD_DIM = 128
Q_LORA_RANK = 384
KV_LORA_RANK = 256
ROPE_THETA = 10000.0
N_EXPERTS = 16
N_GROUPS = 4
EXPERTS_PER_GROUP = 4
D_FF = 512
DEEPNORM_ALPHA = (2 * DEPTH) ** 0.25
LN_EPS = 1e-5
RMS_EPS = 1e-6

LANES = 128
QK_PAD = 256
N_PAIRS = 6
N_CLASSES = N_GROUPS * N_PAIRS
PAIRS = ((0, 1), (0, 2), (0, 3), (1, 2), (1, 3), (2, 3))

TM_TOKEN = 512
TM_MOE = 256
TQ = 256
TK = 512
DFT_N1 = 64
DFT_COLS = 4096
DFT_KB = 8
ROW_CHUNK = 2048
VMEM_LIMIT = 48 * 1024 * 1024

SOFTMAX_SCALE = (QK_NOPE_DIM + QK_ROPE_DIM) ** -0.5
LOG2E = math.log2(math.e)


def _cparams(sem):
    return pltpu.CompilerParams(dimension_semantics=sem, vmem_limit_bytes=VMEM_LIMIT)


def _dot(a, b):
    return jnp.dot(a, b, preferred_element_type=F32)


def _inproj_body(x_ref, win_ref, qg_ref, kvg_ref, wqa_ref, wqb_ref, wkv_ref, cos_ref, sin_ref,
                 uf_ref, q_ref, k_ref, v_ref):
    x = x_ref[...]
    h = _dot(x.astype(BF16), win_ref[...])
    uf_ref[...] = h[:, :FOURIER_WIDTH].astype(BF16)

    def rms(c, g):
        return (c * lax.rsqrt(jnp.mean(c * c, axis=-1, keepdims=True) + RMS_EPS) * g).astype(BF16)

    c_q = rms(h[:, FOURIER_WIDTH:FOURIER_WIDTH + Q_LORA_RANK], qg_ref[...])
    kv_lo = FOURIER_WIDTH + Q_LORA_RANK
    c_kv = rms(h[:, kv_lo:kv_lo + KV_LORA_RANK], kvg_ref[...])
    cos = cos_ref[...]
    sin = sin_ref[...]
    t = h[:, kv_lo + KV_LORA_RANK:]
    k_rope = (t * cos + pltpu.roll(t, 64, 1) * sin).astype(BF16)
    qa = _dot(c_q, wqa_ref[...])
    qb = _dot(c_q, wqb_ref[...])
    kv = _dot(c_kv, wkv_ref[...])
    qs = SOFTMAX_SCALE * LOG2E
    for hd in range(N_HEADS):
        lo = hd * QK_PAD
        q_ref[hd, :, :LANES] = (qa[:, lo:lo + LANES] * qs).astype(BF16)
        q_rope = qa[:, lo + LANES:lo + QK_PAD] * cos + qb[:, hd * LANES:(hd + 1) * LANES] * sin
        q_ref[hd, :, LANES:] = (q_rope * qs).astype(BF16)
        k_ref[hd, :, :LANES] = kv[:, lo:lo + LANES].astype(BF16)
        k_ref[hd, :, LANES:] = k_rope
        v_ref[hd] = kv[:, lo + LANES:lo + QK_PAD].astype(BF16)


def _inproj(x, lw, cos_t, sin_t):
    B, S, D = x.shape
    tm = min(TM_TOKEN, S)
    full = lambda a: pl.BlockSpec(a.shape, lambda b, i: (0,) * a.ndim)
    weights = (lw["w_in"], lw["q_g"], lw["kv_g"], lw["wqa"], lw["wqb"], lw["wkv"])
    return pl.pallas_call(
        _inproj_body,
        grid=(B, S // tm),
        in_specs=[pl.BlockSpec((None, tm, D), lambda b, i: (b, i, 0))]
        + [full(w) for w in weights]
        + [pl.BlockSpec((tm, LANES), lambda b, i: (i, 0))] * 2,
        out_specs=[
            pl.BlockSpec((None, tm, FOURIER_WIDTH), lambda b, i: (b, i, 0)),
            pl.BlockSpec((None, N_HEADS, tm, QK_PAD), lambda b, i: (b, 0, i, 0)),
            pl.BlockSpec((None, N_HEADS, tm, QK_PAD), lambda b, i: (b, 0, i, 0)),
            pl.BlockSpec((None, N_HEADS, tm, V_HEAD_DIM), lambda b, i: (b, 0, i, 0)),
        ],
        out_shape=[
            jax.ShapeDtypeStruct((B, S, FOURIER_WIDTH), BF16),
            jax.ShapeDtypeStruct((B, N_HEADS, S, QK_PAD), BF16),
            jax.ShapeDtypeStruct((B, N_HEADS, S, QK_PAD), BF16),
            jax.ShapeDtypeStruct((B, N_HEADS, S, V_HEAD_DIM), BF16),
        ],
        compiler_params=_cparams(("parallel", "parallel")),
        name="inproj",
    )(x, *weights, cos_t, sin_t)


def _dft1_body(u_ref, w1_ref, tc_ref, ts_ref, z_ref):
    n1 = u_ref.shape[0]
    y = _dot(w1_ref[...], u_ref[...])
    yr, yi = y[:n1], y[n1:]
    tc, ts = tc_ref[...], ts_ref[...]
    z_ref[0] = (yr * tc + yi * ts).astype(BF16)
    z_ref[1] = (yi * tc - yr * ts).astype(BF16)


def _dft2_body(z_ref, w2a_ref, w2b_ref, cbd_ref, sbd_ref, wbd_ref, f_ref, *, norm):
    n2 = z_ref.shape[2]
    for j in range(z_ref.shape[1]):
        v = _dot(w2a_ref[...], z_ref[0, j]) + _dot(w2b_ref[...], z_ref[1, j])
        vr, vi = v[:n2].astype(BF16), v[n2:].astype(BF16)
        g = (_dot(vr, cbd_ref[...]) + _dot(vi, sbd_ref[...])) * norm
        f_ref[j] = _dot(g.astype(BF16), wbd_ref[...]).astype(BF16)


def _dft_tables(S):
    n1, n2 = DFT_N1, S // DFT_N1

    def cs(n, rows, cols):
        ang = 2.0 * np.pi * ((np.outer(rows, cols)) % n) / n
        return np.cos(ang), np.sin(ang)

    c1, s1 = cs(n1, np.arange(n1), np.arange(n1))
    w1 = np.concatenate([c1, -s1], axis=0)
    tc, ts = cs(S, np.arange(n1), np.arange(n2))
    tc = np.repeat(tc, FOURIER_WIDTH, axis=1)
    ts = np.repeat(ts, FOURIER_WIDTH, axis=1)
    c2, s2 = cs(n2, np.arange(n2), np.arange(n2))
    w2a = np.concatenate([c2, -s2], axis=0)
    w2b = np.concatenate([s2, c2], axis=0)
    cg, sg = cs(FOURIER_GROUP_DIM, np.arange(FOURIER_GROUP_DIM), np.arange(FOURIER_GROUP_DIM))
    eye = np.eye(FOURIER_GROUPS)
    return dict(
        w1=jnp.asarray(w1, BF16), tc=jnp.asarray(tc, F32), ts=jnp.asarray(ts, F32),
        w2a=jnp.asarray(w2a, BF16), w2b=jnp.asarray(w2b, BF16),
        cbd=jnp.asarray(np.kron(eye, cg), BF16), sbd=jnp.asarray(np.kron(eye, sg), BF16),
    )


def _fourier(u_f, wbd, tabs):
    B, S, W = u_f.shape
    n1, n2 = DFT_N1, S // DFT_N1
    cols = min(DFT_COLS, n2 * W)
    u2 = u_f.reshape(B, n1, n2 * W)
    z = pl.pallas_call(
        _dft1_body,
        grid=(B, n2 * W // cols),
        in_specs=[
            pl.BlockSpec((None, n1, cols), lambda b, c: (b, 0, c)),
            pl.BlockSpec((2 * n1, n1), lambda b, c: (0, 0)),
            pl.BlockSpec((n1, cols), lambda b, c: (0, c)),
            pl.BlockSpec((n1, cols), lambda b, c: (0, c)),
        ],
        out_specs=pl.BlockSpec((None, 2, n1, cols), lambda b, c: (b, 0, 0, c)),
        out_shape=jax.ShapeDtypeStruct((B, 2, n1, n2 * W), BF16),
        compiler_params=_cparams(("parallel", "parallel")),
        name="dft_stage1",
    )(u2, tabs["w1"], tabs["tc"], tabs["ts"])
    z5 = z.reshape(B, 2, n1, n2, W)
    kb = DFT_KB
    f4 = pl.pallas_call(
        functools.partial(_dft2_body, norm=float((S * FOURIER_GROUP_DIM) ** -0.5)),
        grid=(B, n1 // kb),
        in_specs=[
            pl.BlockSpec((None, 2, kb, n2, W), lambda b, k: (b, 0, k, 0, 0)),
            pl.BlockSpec((2 * n2, n2), lambda b, k: (0, 0)),
            pl.BlockSpec((2 * n2, n2), lambda b, k: (0, 0)),
            pl.BlockSpec((W, W), lambda b, k: (0, 0)),
            pl.BlockSpec((W, W), lambda b, k: (0, 0)),
            pl.BlockSpec((W, W), lambda b, k: (0, 0)),
        ],
        out_specs=pl.BlockSpec((None, kb, n2, W), lambda b, k: (b, k, 0, 0)),
        out_shape=jax.ShapeDtypeStruct((B, n1, n2, W), BF16),
        compiler_params=_cparams(("parallel", "parallel")),
        name="dft_stage2",
    )(z5, tabs["w2a"], tabs["w2b"], tabs["cbd"], tabs["sbd"], wbd)
    return jnp.transpose(f4, (0, 2, 1, 3)).reshape(B, S, W)


def _attn_body(q_ref, k_ref, v_ref, o_ref, *, tk):
    q = q_ref[...]
    tq = q.shape[0]
    n_kv = k_ref.shape[0] // tk

    def step(j, carry):
        m, l, acc = carry
        off = pl.multiple_of(j * tk, tk)
        kj = k_ref[pl.ds(off, tk), :]
        vj = v_ref[pl.ds(off, tk), :]
        s = lax.dot_general(q, kj, (((1,), (1,)), ((), ())), preferred_element_type=F32)
        m_new = jnp.maximum(m, jnp.max(s, axis=-1, keepdims=True))
        alpha = jnp.exp2(m - m_new)
        p = jnp.exp2(s - m_new)
        l = alpha * l + jnp.sum(p, axis=-1, keepdims=True)
        acc = alpha * acc + _dot(p.astype(BF16), vj)
        return m_new, l, acc

    init = (jnp.full((tq, 1), -jnp.inf, F32), jnp.zeros((tq, 1), F32),
            jnp.zeros((tq, V_HEAD_DIM), F32))
    _, l, acc = lax.fori_loop(0, n_kv, step, init)
    o_ref[...] = (acc / l).astype(BF16)


def _attention(q, k, v):
    B, H, S, _ = q.shape
    tq, tk = min(TQ, S), min(TK, S)
    return pl.pallas_call(
        functools.partial(_attn_body, tk=tk),
        grid=(B, H, S // tq),
        in_specs=[
            pl.BlockSpec((None, None, tq, QK_PAD), lambda b, h, i: (b, h, i, 0)),
            pl.BlockSpec((None, None, S, QK_PAD), lambda b, h, i: (b, h, 0, 0)),
            pl.BlockSpec((None, None, S, V_HEAD_DIM), lambda b, h, i: (b, h, 0, 0)),
        ],
        out_specs=pl.BlockSpec((None, tq, V_HEAD_DIM), lambda b, h, i: (b, i, h)),
        out_shape=jax.ShapeDtypeStruct((B, S, H * V_HEAD_DIM), BF16),
        compiler_params=_cparams(("parallel", "parallel", "parallel")),
        name="attention",
    )(q, k, v)


def _layernorm(y, g, b):
    mu = jnp.mean(y, axis=-1, keepdims=True)
    d = y - mu
    var = jnp.mean(d * d, axis=-1, keepdims=True)
    return d * lax.rsqrt(var + LN_EPS) * g + b


def _sigmoid(x):
    return 1.0 / (1.0 + jnp.exp(-x))


def _outproj_body(x_ref, f_ref, a_ref, wf_ref, wa_ref, g_ref, b_ref, wra_ref, wrb_ref, bias_ref,
                  tril_ref, x1_ref, meta_ref, cnt_ref, run_ref):
    @pl.when(pl.program_id(0) == 0)
    def _():
        run_ref[...] = jnp.zeros_like(run_ref)

    mix = _dot(f_ref[...], wf_ref[...]) + _dot(a_ref[...], wa_ref[...])
    x1 = _layernorm(DEEPNORM_ALPHA * x_ref[...] + mix, g_ref[...], b_ref[...])
    x1_ref[...] = x1
    xb = x1.astype(BF16)
    pair_sum = _sigmoid(_dot(xb, wra_ref[...])) + _sigmoid(_dot(xb, wrb_ref[...])) + bias_ref[...]
    lane = lax.broadcasted_iota(jnp.int32, pair_sum.shape, 1)
    best = jnp.max(pair_sum, axis=-1, keepdims=True)
    cls = jnp.min(jnp.where(pair_sum == best, lane, LANES), axis=-1, keepdims=True)
    onehot = lane == cls
    prefix = _dot(tril_ref[...], onehot.astype(BF16))
    run = run_ref[...]
    rank = jnp.sum(jnp.where(onehot, prefix + run, 0.0), axis=-1, keepdims=True) - 1.0
    meta_ref[...] = jnp.where(lane == 0, cls, jnp.where(lane == 1, rank.astype(jnp.int32), 0))
    run = run + prefix[prefix.shape[0] - 1:, :]
    run_ref[...] = run
    cnt_ref[...] = run


def _outproj(x2d, f2d, a2d, lw, rw):
    T, D = x2d.shape
    tm = min(TM_TOKEN, T)
    tril = jnp.asarray(np.tril(np.ones((tm, tm), np.float32)), BF16)
    full = lambda a: pl.BlockSpec(a.shape, lambda i: (0,) * a.ndim)
    row = lambda w: pl.BlockSpec((tm, w), lambda i: (i, 0))
    consts = (lw["w_out_f"], lw["w_out_a"], lw["ln1_g"], lw["ln1_b"], rw["wra"], rw["wrb"],
              rw["bias_ab"], tril)
    return pl.pallas_call(
        _outproj_body,
        grid=(T // tm,),
        in_specs=[row(D), row(FOURIER_WIDTH), row(N_HEADS * V_HEAD_DIM)] + [full(c) for c in consts],
        out_specs=[row(D), row(LANES), pl.BlockSpec((1, LANES), lambda i: (0, 0))],
        out_shape=[
            jax.ShapeDtypeStruct((T, D), F32),
            jax.ShapeDtypeStruct((T, LANES), jnp.int32),
            jax.ShapeDtypeStruct((1, LANES), F32),
        ],
        scratch_shapes=[pltpu.VMEM((1, LANES), F32)],
        compiler_params=_cparams(("arbitrary",)),
        name="outproj_route",
    )(x2d, f2d, a2d, *consts)


def _row_copy(src_hbm, dst_hbm, src_row, dst_row, sem):
    return pltpu.make_async_copy(src_hbm.at[pl.ds(src_row, 1)], dst_hbm.at[pl.ds(dst_row, 1)], sem)


def _scatter_rows_body(pos_ref, src_hbm, init_hbm, dst_hbm, sem, *, chunk):
    del init_hbm
    base = pl.program_id(0) * chunk

    def issue(j, c):
        _row_copy(src_hbm, dst_hbm, base + j, pos_ref[j], sem).start()
        return c

    def drain(j, c):
        _row_copy(src_hbm, dst_hbm, base + j, pos_ref[j], sem).wait()
        return c

    lax.fori_loop(0, chunk, issue, 0)
    lax.fori_loop(0, chunk, drain, 0)


def _gather_rows_body(pos_ref, src_hbm, dst_hbm, sem, *, chunk):
    base = pl.program_id(0) * chunk

    def issue(j, c):
        _row_copy(src_hbm, dst_hbm, pos_ref[j], base + j, sem).start()
        return c

    def drain(j, c):
        _row_copy(src_hbm, dst_hbm, pos_ref[j], base + j, sem).wait()
        return c

    lax.fori_loop(0, chunk, issue, 0)
    lax.fori_loop(0, chunk, drain, 0)


def _scatter_rows(src, pos, n_dst):
    T, D = src.shape
    chunk = min(ROW_CHUNK, T)
    return pl.pallas_call(
        functools.partial(_scatter_rows_body, chunk=chunk),
        grid=(T // chunk,),
        in_specs=[
            pl.BlockSpec((chunk,), lambda i: (i,), memory_space=pltpu.SMEM),
            pl.BlockSpec(memory_space=pl.ANY),
            pl.BlockSpec(memory_space=pl.ANY),
        ],
        out_specs=pl.BlockSpec(memory_space=pl.ANY),
        out_shape=jax.ShapeDtypeStruct((n_dst, D), src.dtype),
        scratch_shapes=[pltpu.SemaphoreType.DMA(())],
        input_output_aliases={2: 0},
        compiler_params=_cparams(("arbitrary",)),
        name="scatter_rows",
    )(pos, src, jnp.zeros((n_dst, D), src.dtype))


def _gather_rows(src, pos):
    T = pos.shape[0]
    D = src.shape[1]
    chunk = min(ROW_CHUNK, T)
    return pl.pallas_call(
        functools.partial(_gather_rows_body, chunk=chunk),
        grid=(T // chunk,),
        in_specs=[
            pl.BlockSpec((chunk,), lambda i: (i,), memory_space=pltpu.SMEM),
            pl.BlockSpec(memory_space=pl.ANY),
        ],
        out_specs=pl.BlockSpec(memory_space=pl.ANY),
        out_shape=jax.ShapeDtypeStruct((T, D), src.dtype),
        scratch_shapes=[pltpu.SemaphoreType.DMA(())],
        compiler_params=_cparams(("arbitrary",)),
        name="gather_rows",
    )(pos, src)


def _swiglu(xb, wg, wu, wd):
    g = _dot(xb, wg)
    h = (g * _sigmoid(g)) * _dot(xb, wu)
    return _dot(h.astype(BF16), wd)


def _moe_body(ea_ref, eb_ref, nused_ref, x_ref, wr_ref, wsg_ref, wsu_ref, wsd_ref,
              wga_ref, wua_ref, wda_ref, wgb_ref, wub_ref, wdb_ref, g_ref, b_ref, y_ref):
    i = pl.program_id(0)

    @pl.when(i < nused_ref[0])
    def _():
        x = x_ref[...]
        xb = x.astype(BF16)
        scores = _sigmoid(_dot(xb, wr_ref[...]))
        lane = lax.broadcasted_iota(jnp.int32, scores.shape, 1)
        sa = jnp.sum(jnp.where(lane == ea_ref[i], scores, 0.0), axis=-1, keepdims=True)
        sb = jnp.sum(jnp.where(lane == eb_ref[i], scores, 0.0), axis=-1, keepdims=True)
        inv = 1.0 / (sa + sb)
        y = _swiglu(xb, wsg_ref[...], wsu_ref[...], wsd_ref[...])
        y = y + (sa * inv) * _swiglu(xb, wga_ref[...], wua_ref[...], wda_ref[...])
        y = y + (sb * inv) * _swiglu(xb, wgb_ref[...], wub_ref[...], wdb_ref[...])
        y_ref[...] = _layernorm(DEEPNORM_ALPHA * x + y, g_ref[...], b_ref[...])

    @pl.when(i >= nused_ref[0])
    def _():
        y_ref[...] = jnp.zeros_like(y_ref)


def _moe(xs, ea, eb, nused, lw, rw):
    n_rows, D = xs.shape
    tm = TM_MOE
    full = lambda a: pl.BlockSpec(a.shape, lambda i, ea, eb, nu: (0,) * a.ndim)
    exp_a = lambda a: pl.BlockSpec((None,) + a.shape[1:], lambda i, ea, eb, nu: (ea[i], 0, 0))
    exp_b = lambda a: pl.BlockSpec((None,) + a.shape[1:], lambda i, ea, eb, nu: (eb[i], 0, 0))
    row = pl.BlockSpec((tm, D), lambda i, ea, eb, nu: (i, 0))
    shared = (rw["wr_pad"], lw["ws_gate"], lw["ws_up"], lw["ws_down"])
    routed = (lw["w_gate"], lw["w_up"], lw["w_down"])
    grid_spec = pltpu.PrefetchScalarGridSpec(
        num_scalar_prefetch=3,
        grid=(n_rows // tm,),
        in_specs=[row] + [full(w) for w in shared] + [exp_a(w) for w in routed]
        + [exp_b(w) for w in routed] + [full(lw["ln2_g"]), full(lw["ln2_b"])],
        out_specs=row,
    )
    return pl.pallas_call(
        _moe_body,
        grid_spec=grid_spec,
        out_shape=jax.ShapeDtypeStruct((n_rows, D), F32),
        compiler_params=_cparams(("arbitrary",)),
        name="moe_ln2",
    )(ea, eb, nused, xs, *shared, *routed, *routed, lw["ln2_g"], lw["ln2_b"])


def _rotate_half_cols(w):
    half = w.shape[1] // 2
    return jnp.concatenate([-w[:, half:], w[:, :half]], axis=1)


def _prep_layer(i, p):
    w_in = p["w_in"][i]
    k_r_lo = FOURIER_WIDTH + Q_LORA_RANK + KV_LORA_RANK
    w_in_ext = jnp.concatenate([w_in, _rotate_half_cols(w_in[:, k_r_lo:])], axis=1)
    wq = p["w_q_up"][i].reshape(Q_LORA_RANK, N_HEADS, QK_NOPE_DIM + QK_ROPE_DIM)
    zeros = jnp.zeros((Q_LORA_RANK, N_HEADS, QK_PAD - QK_NOPE_DIM - QK_ROPE_DIM), F32)
    wqa = jnp.concatenate([wq, zeros], axis=2).reshape(Q_LORA_RANK, N_HEADS * QK_PAD)
    rot = jnp.concatenate([-wq[:, :, QK_NOPE_DIM + QK_ROPE_DIM // 2:],
                           wq[:, :, QK_NOPE_DIM:QK_NOPE_DIM + QK_ROPE_DIM // 2], zeros], axis=2)
    wqb = rot.reshape(Q_LORA_RANK, N_HEADS * LANES)
    eye = jnp.eye(FOURIER_GROUPS, dtype=F32)
    wbd = (eye[:, None, :, None] * p["w_fourier"][i][:, :, None, :]).reshape(FOURIER_WIDTH, FOURIER_WIDTH)
    row = lambda v: v[i].reshape(1, -1).astype(F32)
    return dict(
        w_in=w_in_ext.astype(BF16), q_g=row(p["q_norm_g"]), kv_g=row(p["kv_norm_g"]),
        wqa=wqa.astype(BF16), wqb=wqb.astype(BF16), wkv=p["w_kv_up"][i].astype(BF16),
        wbd=wbd.astype(BF16),
        w_out_f=p["w_out"][i][:FOURIER_WIDTH].astype(BF16),
        w_out_a=p["w_out"][i][FOURIER_WIDTH:].astype(BF16),
        ln1_g=row(p["ln1_g"]), ln1_b=row(p["ln1_b"]), ln2_g=row(p["ln2_g"]), ln2_b=row(p["ln2_b"]),
        w_gate=p["w_gate"][i].astype(BF16), w_up=p["w_up"][i].astype(BF16),
        w_down=p["w_down"][i].astype(BF16),
        ws_gate=p["ws_gate"][i].astype(BF16), ws_up=p["ws_up"][i].astype(BF16),
        ws_down=p["ws_down"][i].astype(BF16),
    )


def _class_members():
    a = np.array([EXPERTS_PER_GROUP * g + PAIRS[q][0] for g in range(N_GROUPS) for q in range(N_PAIRS)])
    b = np.array([EXPERTS_PER_GROUP * g + PAIRS[q][1] for g in range(N_GROUPS) for q in range(N_PAIRS)])
    return a, b


def _prep_router(w_router, router_bias):
    a, b = _class_members()
    pad = jnp.zeros((D_MODEL, LANES - N_CLASSES), F32)
    wra = jnp.concatenate([w_router[:, a], pad], axis=1)
    wrb = jnp.concatenate([w_router[:, b], pad], axis=1)
    bias = router_bias.astype(F32)
    bias_ab = jnp.concatenate([bias[a] + bias[b], jnp.full((LANES - N_CLASSES,), -jnp.inf, F32)])
    wr_pad = jnp.concatenate([w_router, jnp.zeros((D_MODEL, LANES - N_EXPERTS), F32)], axis=1)
    return dict(wra=wra.astype(BF16), wrb=wrb.astype(BF16), bias_ab=bias_ab.reshape(1, LANES),
                wr_pad=wr_pad.astype(BF16))


def _rope_tables(S):
    pos = jnp.arange(S, dtype=F32)
    inv_freq = ROPE_THETA ** (-jnp.arange(0, QK_ROPE_DIM, 2, dtype=F32) / QK_ROPE_DIM)
    ang = pos[:, None] * inv_freq[None, :]
    zeros = jnp.zeros((S, LANES - QK_ROPE_DIM), F32)
    cos_t = jnp.concatenate([jnp.cos(ang), jnp.cos(ang), zeros], axis=1)
    sin_t = jnp.concatenate([jnp.sin(ang), jnp.sin(ang), zeros], axis=1)
    return cos_t, sin_t


def _dispatch_plan(meta, counts, T):
    tm = TM_MOE
    cls, rank = meta[:, 0], meta[:, 1]
    cnt = counts[0, :N_CLASSES].astype(jnp.int32)
    padded = ((cnt + tm - 1) // tm) * tm
    ends = jnp.cumsum(padded)
    starts = ends - padded
    pos = starts[cls] + rank
    n_tiles = T // tm + N_CLASSES
    nused = ends[-1] // tm
    tile_row = jnp.minimum(jnp.arange(n_tiles, dtype=jnp.int32), nused - 1) * tm
    tile_cls = jnp.minimum(jnp.searchsorted(ends, tile_row, side="right"), N_CLASSES - 1)
    a, b = _class_members()
    ea = jnp.asarray(a, jnp.int32)[tile_cls]
    eb = jnp.asarray(b, jnp.int32)[tile_cls]
    return pos.astype(jnp.int32), ea, eb, nused.reshape(1).astype(jnp.int32), n_tiles * tm


def _layer(x, lw, rw, rope, tabs):
    B, S, D = x.shape
    T = B * S
    u_f, q, k, v = _inproj(x, lw, *rope)
    f = _fourier(u_f, lw["wbd"], tabs)
    a = _attention(q, k, v)
    x1, meta, counts = _outproj(x.reshape(T, D), f.reshape(T, -1), a.reshape(T, -1), lw, rw)
    pos, ea, eb, nused, n_rows = _dispatch_plan(meta, counts, T)
    xs = _scatter_rows(x1, pos, n_rows)
    ys = _moe(xs, ea, eb, nused, lw, rw)
    return _gather_rows(ys, pos).reshape(B, S, D)


def _trunk(x, layers, rw):
    S = x.shape[1]
    rope = _rope_tables(S)
    tabs = _dft_tables(S)
    for lw in layers:
        x = _layer(x, lw, rw, rope, tabs)
    return x


def kernel(x_prompt, x_sample, w_in, q_norm_g, w_q_up, kv_norm_g, w_kv_up, w_fourier, w_out,
           ln1_g, ln1_b, w_router, router_bias, w_gate, w_up, w_down, ws_gate, ws_up, ws_down,
           ln2_g, ln2_b):
    p = dict(w_in=w_in, q_norm_g=q_norm_g, w_q_up=w_q_up, kv_norm_g=kv_norm_g, w_kv_up=w_kv_up,
             w_fourier=w_fourier, w_out=w_out, ln1_g=ln1_g, ln1_b=ln1_b, w_gate=w_gate, w_up=w_up,
             w_down=w_down, ws_gate=ws_gate, ws_up=ws_up, ws_down=ws_down, ln2_g=ln2_g, ln2_b=ln2_b)
    layers = [_prep_layer(i, p) for i in range(w_in.shape[0])]
    rw = _prep_router(w_router, router_bias)
    return _trunk(x_prompt, layers, rw), _trunk(x_sample, layers, rw)
```

```python
import functools
import math

import numpy as np
import jax
import jax.numpy as jnp
from jax import lax
from jax.experimental import pallas as pl
from jax.experimental.pallas import tpu as pltpu

F32 = jnp.float32
BF16 = jnp.bfloat16

D_MODEL = 1024
DEPTH = 2
FOURIER_WIDTH = 256
FOURIER_GROUPS = 4
FOURIER_GROUP_DIM = 64
N_HEADS = 6
QK_NOPE_DIM = 128
QK_ROPE_DIM = 64
V_HEAD_DIM = 128
Q_LORA_RANK = 384
KV_LORA_RANK = 256
ROPE_THETA = 10000.0
N_EXPERTS = 16
N_GROUPS = 4
EXPERTS_PER_GROUP = 4
D_FF = 512
DEEPNORM_ALPHA = (2 * DEPTH) ** 0.25
LN_EPS = 1e-5
RMS_EPS = 1e-6

LANES = 128
QK_PAD = 256
N_PAIRS = 6
N_CLASSES = N_GROUPS * N_PAIRS
PAIRS = ((0, 1), (0, 2), (0, 3), (1, 2), (1, 3), (2, 3))

TM_TOKEN = 512
TM_MOE = 256
TQ = 1024
TK = 512
KV_UNROLL = 4
ROW_SUBLANES = 8
DFT_N1 = 64
DFT_COLS = 4096
DFT_KB = 8
ROW_CHUNK = 2048
DRAIN_UNROLL = 8
VMEM_LIMIT = 48 * 1024 * 1024

SOFTMAX_SCALE = (QK_NOPE_DIM + QK_ROPE_DIM) ** -0.5
LOG2E = math.log2(math.e)


def _cparams(sem):
    return pltpu.CompilerParams(dimension_semantics=sem, vmem_limit_bytes=VMEM_LIMIT)


def _dot(a, b):
    return jnp.dot(a, b, preferred_element_type=F32)


def _load_rows(ref):
    return jnp.concatenate([ref[:, c, :] for c in range(ROW_SUBLANES)], axis=1)


def _store_rows(ref, val):
    for c in range(ROW_SUBLANES):
        ref[:, c, :] = val[:, c * LANES:(c + 1) * LANES]


def _inproj_body(x_ref, win_ref, qg_ref, kvg_ref, wqa_ref, wqb_ref, wkv_ref, cos_ref, sin_ref,
                 uf_ref, q_ref, k_ref, v_ref):
    x = _load_rows(x_ref)
    h = _dot(x.astype(BF16), win_ref[...])
    uf_ref[...] = h[:, :FOURIER_WIDTH].astype(BF16)

    def rms(c, g):
        return (c * lax.rsqrt(jnp.mean(c * c, axis=-1, keepdims=True) + RMS_EPS) * g).astype(BF16)

    c_q = rms(h[:, FOURIER_WIDTH:FOURIER_WIDTH + Q_LORA_RANK], qg_ref[...])
    kv_lo = FOURIER_WIDTH + Q_LORA_RANK
    c_kv = rms(h[:, kv_lo:kv_lo + KV_LORA_RANK], kvg_ref[...])
    cos = cos_ref[...]
    sin = sin_ref[...]
    t = h[:, kv_lo + KV_LORA_RANK:]
    k_rope = (t * cos + pltpu.roll(t, 64, 1) * sin).astype(BF16)
    qa = _dot(c_q, wqa_ref[...])
    qb = _dot(c_q, wqb_ref[...])
    kv = _dot(c_kv, wkv_ref[...])
    qs = SOFTMAX_SCALE * LOG2E
    for hd in range(N_HEADS):
        lo = hd * QK_PAD
        q_ref[hd, :, :LANES] = (qa[:, lo:lo + LANES] * qs).astype(BF16)
        q_rope = qa[:, lo + LANES:lo + QK_PAD] * cos + qb[:, hd * LANES:(hd + 1) * LANES] * sin
        q_ref[hd, :, LANES:] = (q_rope * qs).astype(BF16)
        k_ref[hd, :, :LANES] = kv[:, lo:lo + LANES].astype(BF16)
        k_ref[hd, :, LANES:] = k_rope
        v_ref[hd] = kv[:, lo + LANES:lo + QK_PAD].astype(BF16)


def _inproj(x, lw, cos_t, sin_t):
    B, S = x.shape[:2]
    tm = min(TM_TOKEN, S)
    full = lambda a: pl.BlockSpec(a.shape, lambda b, i: (0,) * a.ndim)
    weights = (lw["w_in"], lw["q_g"], lw["kv_g"], lw["wqa"], lw["wqb"], lw["wkv"])
    return pl.pallas_call(
        _inproj_body,
        grid=(B, S // tm),
        in_specs=[pl.BlockSpec((None, tm, ROW_SUBLANES, LANES), lambda b, i: (b, i, 0, 0))]
        + [full(w) for w in weights]
        + [pl.BlockSpec((tm, LANES), lambda b, i: (i, 0))] * 2,
        out_specs=[
            pl.BlockSpec((None, tm, FOURIER_WIDTH), lambda b, i: (b, i, 0)),
            pl.BlockSpec((None, N_HEADS, tm, QK_PAD), lambda b, i: (b, 0, i, 0)),
            pl.BlockSpec((None, N_HEADS, tm, QK_PAD), lambda b, i: (b, 0, i, 0)),
            pl.BlockSpec((None, N_HEADS, tm, V_HEAD_DIM), lambda b, i: (b, 0, i, 0)),
        ],
        out_shape=[
            jax.ShapeDtypeStruct((B, S, FOURIER_WIDTH), BF16),
            jax.ShapeDtypeStruct((B, N_HEADS, S, QK_PAD), BF16),
            jax.ShapeDtypeStruct((B, N_HEADS, S, QK_PAD), BF16),
            jax.ShapeDtypeStruct((B, N_HEADS, S, V_HEAD_DIM), BF16),
        ],
        compiler_params=_cparams(("parallel", "parallel")),
        name="inproj",
    )(x, *weights, cos_t, sin_t)


def _dft1_body(u_ref, w1_ref, tc_ref, ts_ref, z_ref):
    n1 = u_ref.shape[0]
    y = _dot(w1_ref[...], u_ref[...])
    yr, yi = y[:n1], y[n1:]
    tc, ts = tc_ref[...], ts_ref[...]
    z_ref[0] = (yr * tc + yi * ts).astype(BF16)
    z_ref[1] = (yi * tc - yr * ts).astype(BF16)


def _dft2_body(z_ref, w2a_ref, w2b_ref, cbd_ref, sbd_ref, wbd_ref, f_ref, *, norm):
    n2 = z_ref.shape[2]
    for j in range(z_ref.shape[1]):
        v = _dot(w2a_ref[...], z_ref[0, j]) + _dot(w2b_ref[...], z_ref[1, j])
        vr, vi = v[:n2].astype(BF16), v[n2:].astype(BF16)
        g = (_dot(vr, cbd_ref[...]) + _dot(vi, sbd_ref[...])) * norm
        f_ref[j] = _dot(g.astype(BF16), wbd_ref[...]).astype(BF16)


def _dft_tables(S):
    n1, n2 = DFT_N1, S // DFT_N1

    def cs(n, rows, cols):
        ang = 2.0 * np.pi * ((np.outer(rows, cols)) % n) / n
        return np.cos(ang), np.sin(ang)

    c1, s1 = cs(n1, np.arange(n1), np.arange(n1))
    w1 = np.concatenate([c1, -s1], axis=0)
    tc, ts = cs(S, np.arange(n1), np.arange(n2))
    tc = np.repeat(tc, FOURIER_WIDTH, axis=1)
    ts = np.repeat(ts, FOURIER_WIDTH, axis=1)
    c2, s2 = cs(n2, np.arange(n2), np.arange(n2))
    w2a = np.concatenate([c2, -s2], axis=0)
    w2b = np.concatenate([s2, c2], axis=0)
    cg, sg = cs(FOURIER_GROUP_DIM, np.arange(FOURIER_GROUP_DIM), np.arange(FOURIER_GROUP_DIM))
    eye = np.eye(FOURIER_GROUPS)
    return dict(
        w1=jnp.asarray(w1, BF16), tc=jnp.asarray(tc, F32), ts=jnp.asarray(ts, F32),
        w2a=jnp.asarray(w2a, BF16), w2b=jnp.asarray(w2b, BF16),
        cbd=jnp.asarray(np.kron(eye, cg), BF16), sbd=jnp.asarray(np.kron(eye, sg), BF16),
    )


def _fourier(u_f, wbd, tabs):
    B, S, W = u_f.shape
    n1, n2 = DFT_N1, S // DFT_N1
    cols = min(DFT_COLS, n2 * W)
    u2 = u_f.reshape(B, n1, n2 * W)
    z = pl.pallas_call(
        _dft1_body,
        grid=(B, n2 * W // cols),
        in_specs=[
            pl.BlockSpec((None, n1, cols), lambda b, c: (b, 0, c)),
            pl.BlockSpec((2 * n1, n1), lambda b, c: (0, 0)),
            pl.BlockSpec((n1, cols), lambda b, c: (0, c)),
            pl.BlockSpec((n1, cols), lambda b, c: (0, c)),
        ],
        out_specs=pl.BlockSpec((None, 2, n1, cols), lambda b, c: (b, 0, 0, c)),
        out_shape=jax.ShapeDtypeStruct((B, 2, n1, n2 * W), BF16),
        compiler_params=_cparams(("parallel", "parallel")),
        name="dft_stage1",
    )(u2, tabs["w1"], tabs["tc"], tabs["ts"])
    z5 = z.reshape(B, 2, n1, n2, W)
    kb = DFT_KB
    f4 = pl.pallas_call(
        functools.partial(_dft2_body, norm=float((S * FOURIER_GROUP_DIM) ** -0.5)),
        grid=(B, n1 // kb),
        in_specs=[
            pl.BlockSpec((None, 2, kb, n2, W), lambda b, k: (b, 0, k, 0, 0)),
            pl.BlockSpec((2 * n2, n2), lambda b, k: (0, 0)),
            pl.BlockSpec((2 * n2, n2), lambda b, k: (0, 0)),
            pl.BlockSpec((W, W), lambda b, k: (0, 0)),
            pl.BlockSpec((W, W), lambda b, k: (0, 0)),
            pl.BlockSpec((W, W), lambda b, k: (0, 0)),
        ],
        out_specs=pl.BlockSpec((None, kb, n2, W), lambda b, k: (b, k, 0, 0)),
        out_shape=jax.ShapeDtypeStruct((B, n1, n2, W), BF16),
        compiler_params=_cparams(("parallel", "parallel")),
        name="dft_stage2",
    )(z5, tabs["w2a"], tabs["w2b"], tabs["cbd"], tabs["sbd"], wbd)
    return jnp.transpose(f4, (0, 2, 1, 3)).reshape(B, S, W)


def _attn_body(q_ref, k_ref, v_ref, o_ref, *, tk):
    q = q_ref[...]
    tq = q.shape[0]
    n_kv = k_ref.shape[0] // tk

    def step(j, carry):
        m, l, acc = carry
        off = pl.multiple_of(j * tk, tk)
        kj = k_ref[pl.ds(off, tk), :]
        vj = v_ref[pl.ds(off, tk), :]
        s = lax.dot_general(q, kj, (((1,), (1,)), ((), ())), preferred_element_type=F32)
        m_new = jnp.maximum(m, jnp.max(s, axis=-1, keepdims=True))
        alpha = jnp.exp2(m - m_new)
        p = jnp.exp2(s - m_new)
        l = alpha * l + jnp.sum(p, axis=-1, keepdims=True)
        acc = alpha * acc + _dot(p.astype(BF16), vj)
        return m_new, l, acc

    init = (jnp.full((tq, 1), -jnp.inf, F32), jnp.zeros((tq, 1), F32),
            jnp.zeros((tq, V_HEAD_DIM), F32))
    _, l, acc = lax.fori_loop(0, n_kv, step, init, unroll=math.gcd(KV_UNROLL, n_kv))
    o_ref[...] = (acc / l).astype(BF16)


def _attention(q, k, v):
    B, H, S, _ = q.shape
    tq, tk = min(TQ, S), min(TK, S)
    return pl.pallas_call(
        functools.partial(_attn_body, tk=tk),
        grid=(B, H, S // tq),
        in_specs=[
            pl.BlockSpec((None, None, tq, QK_PAD), lambda b, h, i: (b, h, i, 0)),
            pl.BlockSpec((None, None, S, QK_PAD), lambda b, h, i: (b, h, 0, 0)),
            pl.BlockSpec((None, None, S, V_HEAD_DIM), lambda b, h, i: (b, h, 0, 0)),
        ],
        out_specs=pl.BlockSpec((None, tq, V_HEAD_DIM), lambda b, h, i: (b, i, h)),
        out_shape=jax.ShapeDtypeStruct((B, S, H * V_HEAD_DIM), BF16),
        compiler_params=_cparams(("parallel", "parallel", "parallel")),
        name="attention",
    )(q, k, v)


def _layernorm(y, g, b):
    mu = jnp.mean(y, axis=-1, keepdims=True)
    d = y - mu
    var = jnp.mean(d * d, axis=-1, keepdims=True)
    return d * lax.rsqrt(var + LN_EPS) * g + b


def _sigmoid(x):
    return 1.0 / (1.0 + jnp.exp(-x))


def _outproj_body(x_ref, f_ref, a_ref, wf_ref, wa_ref, g_ref, b_ref, wra_ref, wrb_ref, bias_ref,
                  tril_ref, x1_ref, meta_ref, cnt_ref, run_ref):
    @pl.when(pl.program_id(0) == 0)
    def _():
        run_ref[...] = jnp.zeros_like(run_ref)

    mix = _dot(f_ref[...], wf_ref[...]) + _dot(a_ref[...], wa_ref[...])
    x1 = _layernorm(DEEPNORM_ALPHA * _load_rows(x_ref) + mix, g_ref[...], b_ref[...])
    _store_rows(x1_ref, x1)
    xb = x1.astype(BF16)
    pair_sum = _sigmoid(_dot(xb, wra_ref[...])) + _sigmoid(_dot(xb, wrb_ref[...])) + bias_ref[...]
    lane = lax.broadcasted_iota(jnp.int32, pair_sum.shape, 1)
    best = jnp.max(pair_sum, axis=-1, keepdims=True)
    cls = jnp.min(jnp.where(pair_sum == best, lane, LANES), axis=-1, keepdims=True)
    onehot = lane == cls
    prefix = _dot(tril_ref[...], onehot.astype(BF16))
    run = run_ref[...]
    rank = jnp.sum(jnp.where(onehot, prefix + run, 0.0), axis=-1, keepdims=True) - 1.0
    meta_ref[...] = jnp.where(lane == 0, cls, jnp.where(lane == 1, rank.astype(jnp.int32), 0))
    run = run + prefix[prefix.shape[0] - 1:, :]
    run_ref[...] = run
    cnt_ref[...] = run


def _outproj(x3d, f2d, a2d, lw, rw):
    T = x3d.shape[0]
    tm = min(TM_TOKEN, T)
    tril = jnp.asarray(np.tril(np.ones((tm, tm), np.float32)), BF16)
    full = lambda a: pl.BlockSpec(a.shape, lambda i: (0,) * a.ndim)
    row = lambda w: pl.BlockSpec((tm, w), lambda i: (i, 0))
    tok = pl.BlockSpec((tm, ROW_SUBLANES, LANES), lambda i: (i, 0, 0))
    consts = (lw["w_out_f"], lw["w_out_a"], lw["ln1_g"], lw["ln1_b"], rw["wra"], rw["wrb"],
              rw["bias_ab"], tril)
    return pl.pallas_call(
        _outproj_body,
        grid=(T // tm,),
        in_specs=[tok, row(FOURIER_WIDTH), row(N_HEADS * V_HEAD_DIM)] + [full(c) for c in consts],
        out_specs=[tok, row(LANES), pl.BlockSpec((1, LANES), lambda i: (0, 0))],
        out_shape=[
            jax.ShapeDtypeStruct((T, ROW_SUBLANES, LANES), F32),
            jax.ShapeDtypeStruct((T, LANES), jnp.int32),
            jax.ShapeDtypeStruct((1, LANES), F32),
        ],
        scratch_shapes=[pltpu.VMEM((1, LANES), F32)],
        compiler_params=_cparams(("arbitrary",)),
        name="outproj_route",
    )(x3d, f2d, a2d, *consts)


def _row_copy(src_hbm, dst_hbm, src_row, dst_row, sem):
    return pltpu.make_async_copy(src_hbm.at[pl.ds(src_row, 1)], dst_hbm.at[pl.ds(dst_row, 1)], sem)


def _scatter_rows_body(pos_ref, src_hbm, init_hbm, dst_hbm, sem, *, chunk):
    del init_hbm
    base = pl.program_id(0) * chunk

    def issue(j, c):
        _row_copy(src_hbm, dst_hbm, base + j, pos_ref[j], sem).start()
        return c

    def drain(j, c):
        _row_copy(src_hbm, dst_hbm, base + j, pos_ref[j], sem).wait()
        return c

    lax.fori_loop(0, chunk, issue, 0)
    lax.fori_loop(0, chunk, drain, 0, unroll=DRAIN_UNROLL)


def _gather_rows_body(pos_ref, src_hbm, dst_hbm, sem, *, chunk):
    base = pl.program_id(0) * chunk

    def issue(j, c):
        _row_copy(src_hbm, dst_hbm, pos_ref[j], base + j, sem).start()
        return c

    def drain(j, c):
        _row_copy(src_hbm, dst_hbm, pos_ref[j], base + j, sem).wait()
        return c

    lax.fori_loop(0, chunk, issue, 0)
    lax.fori_loop(0, chunk, drain, 0, unroll=DRAIN_UNROLL)


def _scatter_rows(src, pos, n_dst):
    T = src.shape[0]
    chunk = min(ROW_CHUNK, T)
    return pl.pallas_call(
        functools.partial(_scatter_rows_body, chunk=chunk),
        grid=(T // chunk,),
        in_specs=[
            pl.BlockSpec((chunk,), lambda i: (i,), memory_space=pltpu.SMEM),
            pl.BlockSpec(memory_space=pl.ANY),
            pl.BlockSpec(memory_space=pl.ANY),
        ],
        out_specs=pl.BlockSpec(memory_space=pl.ANY),
        out_shape=jax.ShapeDtypeStruct((n_dst,) + src.shape[1:], src.dtype),
        scratch_shapes=[pltpu.SemaphoreType.DMA(())],
        input_output_aliases={2: 0},
        compiler_params=_cparams(("arbitrary",)),
        name="scatter_rows",
    )(pos, src, jnp.zeros((n_dst,) + src.shape[1:], src.dtype))


def _gather_rows(src, pos):
    T = pos.shape[0]
    chunk = min(ROW_CHUNK, T)
    return pl.pallas_call(
        functools.partial(_gather_rows_body, chunk=chunk),
        grid=(T // chunk,),
        in_specs=[
            pl.BlockSpec((chunk,), lambda i: (i,), memory_space=pltpu.SMEM),
            pl.BlockSpec(memory_space=pl.ANY),
        ],
        out_specs=pl.BlockSpec(memory_space=pl.ANY),
        out_shape=jax.ShapeDtypeStruct((T,) + src.shape[1:], src.dtype),
        scratch_shapes=[pltpu.SemaphoreType.DMA(())],
        compiler_params=_cparams(("arbitrary",)),
        name="gather_rows",
    )(pos, src)


def _swiglu(xb, wg, wu, wd):
    g = _dot(xb, wg)
    h = (g * _sigmoid(g)) * _dot(xb, wu)
    return _dot(h.astype(BF16), wd)


def _moe_body(ea_ref, eb_ref, nused_ref, x_ref, wr_ref, wsg_ref, wsu_ref, wsd_ref,
              wga_ref, wua_ref, wda_ref, wgb_ref, wub_ref, wdb_ref, g_ref, b_ref, y_ref):
    i = pl.program_id(0)

    @pl.when(i < nused_ref[0])
    def _():
        x = _load_rows(x_ref)
        xb = x.astype(BF16)
        scores = _sigmoid(_dot(xb, wr_ref[...]))
        lane = lax.broadcasted_iota(jnp.int32, scores.shape, 1)
        sa = jnp.sum(jnp.where(lane == ea_ref[i], scores, 0.0), axis=-1, keepdims=True)
        sb = jnp.sum(jnp.where(lane == eb_ref[i], scores, 0.0), axis=-1, keepdims=True)
        inv = 1.0 / (sa + sb)
        y = _swiglu(xb, wsg_ref[...], wsu_ref[...], wsd_ref[...])
        y = y + (sa * inv) * _swiglu(xb, wga_ref[...], wua_ref[...], wda_ref[...])
        y = y + (sb * inv) * _swiglu(xb, wgb_ref[...], wub_ref[...], wdb_ref[...])
        _store_rows(y_ref, _layernorm(DEEPNORM_ALPHA * x + y, g_ref[...], b_ref[...]))

    @pl.when(i >= nused_ref[0])
    def _():
        y_ref[...] = jnp.zeros_like(y_ref)


def _moe(xs, ea, eb, nused, lw, rw):
    n_rows = xs.shape[0]
    tm = TM_MOE
    full = lambda a: pl.BlockSpec(a.shape, lambda i, ea, eb, nu: (0,) * a.ndim)
    exp_a = lambda a: pl.BlockSpec((None,) + a.shape[1:], lambda i, ea, eb, nu: (ea[i], 0, 0))
    exp_b = lambda a: pl.BlockSpec((None,) + a.shape[1:], lambda i, ea, eb, nu: (eb[i], 0, 0))
    row = pl.BlockSpec((tm, ROW_SUBLANES, LANES), lambda i, ea, eb, nu: (i, 0, 0))
    shared = (rw["wr_pad"], lw["ws_gate"], lw["ws_up"], lw["ws_down"])
    routed = (lw["w_gate"], lw["w_up"], lw["w_down"])
    grid_spec = pltpu.PrefetchScalarGridSpec(
        num_scalar_prefetch=3,
        grid=(n_rows // tm,),
        in_specs=[row] + [full(w) for w in shared] + [exp_a(w) for w in routed]
        + [exp_b(w) for w in routed] + [full(lw["ln2_g"]), full(lw["ln2_b"])],
        out_specs=row,
    )
    return pl.pallas_call(
        _moe_body,
        grid_spec=grid_spec,
        out_shape=jax.ShapeDtypeStruct(xs.shape, F32),
        compiler_params=_cparams(("arbitrary",)),
        name="moe_ln2",
    )(ea, eb, nused, xs, *shared, *routed, *routed, lw["ln2_g"], lw["ln2_b"])


def _rotate_half_cols(w):
    half = w.shape[1] // 2
    return jnp.concatenate([-w[:, half:], w[:, :half]], axis=1)


def _prep_layer(i, p):
    w_in = p["w_in"][i]
    k_r_lo = FOURIER_WIDTH + Q_LORA_RANK + KV_LORA_RANK
    w_in_ext = jnp.concatenate([w_in, _rotate_half_cols(w_in[:, k_r_lo:])], axis=1)
    wq = p["w_q_up"][i].reshape(Q_LORA_RANK, N_HEADS, QK_NOPE_DIM + QK_ROPE_DIM)
    zeros = jnp.zeros((Q_LORA_RANK, N_HEADS, QK_PAD - QK_NOPE_DIM - QK_ROPE_DIM), F32)
    wqa = jnp.concatenate([wq, zeros], axis=2).reshape(Q_LORA_RANK, N_HEADS * QK_PAD)
    rot = jnp.concatenate([-wq[:, :, QK_NOPE_DIM + QK_ROPE_DIM // 2:],
                           wq[:, :, QK_NOPE_DIM:QK_NOPE_DIM + QK_ROPE_DIM // 2], zeros], axis=2)
    wqb = rot.reshape(Q_LORA_RANK, N_HEADS * LANES)
    eye = jnp.eye(FOURIER_GROUPS, dtype=F32)
    wbd = (eye[:, None, :, None] * p["w_fourier"][i][:, :, None, :]).reshape(FOURIER_WIDTH, FOURIER_WIDTH)
    row = lambda v: v[i].reshape(1, -1).astype(F32)
    return dict(
        w_in=w_in_ext.astype(BF16), q_g=row(p["q_norm_g"]), kv_g=row(p["kv_norm_g"]),
        wqa=wqa.astype(BF16), wqb=wqb.astype(BF16), wkv=p["w_kv_up"][i].astype(BF16),
        wbd=wbd.astype(BF16),
        w_out_f=p["w_out"][i][:FOURIER_WIDTH].astype(BF16),
        w_out_a=p["w_out"][i][FOURIER_WIDTH:].astype(BF16),
        ln1_g=row(p["ln1_g"]), ln1_b=row(p["ln1_b"]), ln2_g=row(p["ln2_g"]), ln2_b=row(p["ln2_b"]),
        w_gate=p["w_gate"][i].astype(BF16), w_up=p["w_up"][i].astype(BF16),
        w_down=p["w_down"][i].astype(BF16),
        ws_gate=p["ws_gate"][i].astype(BF16), ws_up=p["ws_up"][i].astype(BF16),
        ws_down=p["ws_down"][i].astype(BF16),
    )


def _class_members():
    a = np.array([EXPERTS_PER_GROUP * g + PAIRS[q][0] for g in range(N_GROUPS) for q in range(N_PAIRS)])
    b = np.array([EXPERTS_PER_GROUP * g + PAIRS[q][1] for g in range(N_GROUPS) for q in range(N_PAIRS)])
    return a, b


def _prep_router(w_router, router_bias):
    a, b = _class_members()
    pad = jnp.zeros((D_MODEL, LANES - N_CLASSES), F32)
    wra = jnp.concatenate([w_router[:, a], pad], axis=1)
    wrb = jnp.concatenate([w_router[:, b], pad], axis=1)
    bias = router_bias.astype(F32)
    bias_ab = jnp.concatenate([bias[a] + bias[b], jnp.full((LANES - N_CLASSES,), -jnp.inf, F32)])
    wr_pad = jnp.concatenate([w_router, jnp.zeros((D_MODEL, LANES - N_EXPERTS), F32)], axis=1)
    return dict(wra=wra.astype(BF16), wrb=wrb.astype(BF16), bias_ab=bias_ab.reshape(1, LANES),
                wr_pad=wr_pad.astype(BF16))


def _rope_tables(S):
    pos = jnp.arange(S, dtype=F32)
    inv_freq = ROPE_THETA ** (-jnp.arange(0, QK_ROPE_DIM, 2, dtype=F32) / QK_ROPE_DIM)
    ang = pos[:, None] * inv_freq[None, :]
    zeros = jnp.zeros((S, LANES - QK_ROPE_DIM), F32)
    cos_t = jnp.concatenate([jnp.cos(ang), jnp.cos(ang), zeros], axis=1)
    sin_t = jnp.concatenate([jnp.sin(ang), jnp.sin(ang), zeros], axis=1)
    return cos_t, sin_t


def _dispatch_plan(meta, counts, T):
    tm = TM_MOE
    cls, rank = meta[:, 0], meta[:, 1]
    cnt = counts[0, :N_CLASSES].astype(jnp.int32)
    padded = ((cnt + tm - 1) // tm) * tm
    ends = jnp.cumsum(padded)
    starts = ends - padded
    pos = starts[cls] + rank
    n_tiles = T // tm + N_CLASSES
    nused = ends[-1] // tm
    tile_row = jnp.minimum(jnp.arange(n_tiles, dtype=jnp.int32), nused - 1) * tm
    tile_cls = jnp.sum((ends[None, :] <= tile_row[:, None]).astype(jnp.int32), axis=1)
    tile_cls = jnp.minimum(tile_cls, N_CLASSES - 1)
    a, b = _class_members()
    ea = jnp.asarray(a, jnp.int32)[tile_cls]
    eb = jnp.asarray(b, jnp.int32)[tile_cls]
    return pos.astype(jnp.int32), ea, eb, nused.reshape(1).astype(jnp.int32), n_tiles * tm


def _layer(x, lw, rw, rope, tabs):
    B, S = x.shape[:2]
    T = B * S
    u_f, q, k, v = _inproj(x, lw, *rope)
    f = _fourier(u_f, lw["wbd"], tabs)
    a = _attention(q, k, v)
    x1, meta, counts = _outproj(x.reshape((T,) + x.shape[2:]), f.reshape(T, -1), a.reshape(T, -1),
                                lw, rw)
    pos, ea, eb, nused, n_rows = _dispatch_plan(meta, counts, T)
    xs = _scatter_rows(x1, pos, n_rows)
    ys = _moe(xs, ea, eb, nused, lw, rw)
    return _gather_rows(ys, pos).reshape(x.shape)


def _trunk(x, layers, rw):
    B, S, D = x.shape
    rope = _rope_tables(S)
    tabs = _dft_tables(S)
    x = x.reshape(B, S, ROW_SUBLANES, LANES)
    for lw in layers:
        x = _layer(x, lw, rw, rope, tabs)
    return x.reshape(B, S, D)


def kernel(x_prompt, x_sample, w_in, q_norm_g, w_q_up, kv_norm_g, w_kv_up, w_fourier, w_out,
           ln1_g, ln1_b, w_router, router_bias, w_gate, w_up, w_down, ws_gate, ws_up, ws_down,
           ln2_g, ln2_b):
    p = dict(w_in=w_in, q_norm_g=q_norm_g, w_q_up=w_q_up, kv_norm_g=kv_norm_g, w_kv_up=w_kv_up,
             w_fourier=w_fourier, w_out=w_out, ln1_g=ln1_g, ln1_b=ln1_b, w_gate=w_gate, w_up=w_up,
             w_down=w_down, ws_gate=ws_gate, ws_up=ws_up, ws_down=ws_down, ln2_g=ln2_g, ln2_b=ln2_b)
    layers = [_prep_layer(i, p) for i in range(w_in.shape[0])]
    rw = _prep_router(w_router, router_bias)
    return _trunk(x_prompt, layers, rw), _trunk(x_sample, layers, rw)
```

```python
import functools
import math

import numpy as np
import jax
import jax.numpy as jnp
from jax import lax
from jax.experimental import pallas as pl
from jax.experimental.pallas import tpu as pltpu

F32 = jnp.float32
BF16 = jnp.bfloat16

D_MODEL = 1024
DEPTH = 2
FOURIER_WIDTH = 256
FOURIER_GROUPS = 4
FOURIER_GROUP_DIM = 64
N_HEADS = 6
QK_NOPE_DIM = 128
QK_ROPE_DIM = 64
V_HEAD_DIM = 128
Q_LORA_RANK = 384
KV_LORA_RANK = 256
ROPE_THETA = 10000.0
N_EXPERTS = 16
N_GROUPS = 4
EXPERTS_PER_GROUP = 4
D_FF = 512
DEEPNORM_ALPHA = (2 * DEPTH) ** 0.25
LN_EPS = 1e-5
RMS_EPS = 1e-6

LANES = 128
QK_PAD = 256
N_PAIRS = 6
N_CLASSES = N_GROUPS * N_PAIRS
PAIRS = ((0, 1), (0, 2), (0, 3), (1, 2), (1, 3), (2, 3))

TM_TOKEN = 512
TM_MOE = 256
TQ = 1024
TK = 512
KV_UNROLL = 8
ROW_SUBLANES = 8
DFT_N1 = 64
DFT_COLS = 4096
DFT_KB = 8
ROW_CHUNK = 1024
DRAIN_UNROLL = 8
VMEM_LIMIT = 48 * 1024 * 1024

SOFTMAX_SCALE = (QK_NOPE_DIM + QK_ROPE_DIM) ** -0.5
LOG2E = math.log2(math.e)


def _cparams(sem):
    return pltpu.CompilerParams(dimension_semantics=sem, vmem_limit_bytes=VMEM_LIMIT)


def _dot(a, b):
    return jnp.dot(a, b, preferred_element_type=F32)


def _load_rows(ref):
    return jnp.concatenate([ref[:, c, :] for c in range(ROW_SUBLANES)], axis=1)


def _store_rows(ref, val):
    for c in range(ROW_SUBLANES):
        ref[:, c, :] = val[:, c * LANES:(c + 1) * LANES]


def _inproj_body(x_ref, win_ref, qg_ref, kvg_ref, wqa_ref, wqb_ref, wkv_ref, cos_ref, sin_ref,
                 uf_ref, q_ref, k_ref, v_ref):
    x = _load_rows(x_ref)
    h = _dot(x.astype(BF16), win_ref[...])
    uf_ref[...] = h[:, :FOURIER_WIDTH].astype(BF16)

    def rms(c, g):
        return (c * lax.rsqrt(jnp.mean(c * c, axis=-1, keepdims=True) + RMS_EPS) * g).astype(BF16)

    c_q = rms(h[:, FOURIER_WIDTH:FOURIER_WIDTH + Q_LORA_RANK], qg_ref[...])
    kv_lo = FOURIER_WIDTH + Q_LORA_RANK
    c_kv = rms(h[:, kv_lo:kv_lo + KV_LORA_RANK], kvg_ref[...])
    cos = cos_ref[...]
    sin = sin_ref[...]
    t = h[:, kv_lo + KV_LORA_RANK:]
    k_rope = (t * cos + pltpu.roll(t, 64, 1) * sin).astype(BF16)
    qa = _dot(c_q, wqa_ref[...])
    qb = _dot(c_q, wqb_ref[...])
    kv = _dot(c_kv, wkv_ref[...])
    qs = SOFTMAX_SCALE * LOG2E
    for hd in range(N_HEADS):
        lo = hd * QK_PAD
        q_ref[hd, :, :LANES] = (qa[:, lo:lo + LANES] * qs).astype(BF16)
        q_rope = qa[:, lo + LANES:lo + QK_PAD] * cos + qb[:, hd * LANES:(hd + 1) * LANES] * sin
        q_ref[hd, :, LANES:] = (q_rope * qs).astype(BF16)
        k_ref[hd, :, :LANES] = kv[:, lo:lo + LANES].astype(BF16)
        k_ref[hd, :, LANES:] = k_rope
        v_ref[hd] = kv[:, lo + LANES:lo + QK_PAD].astype(BF16)


def _inproj(x, lw, cos_t, sin_t):
    B, S = x.shape[:2]
    tm = min(TM_TOKEN, S)
    full = lambda a: pl.BlockSpec(a.shape, lambda b, i: (0,) * a.ndim)
    weights = (lw["w_in"], lw["q_g"], lw["kv_g"], lw["wqa"], lw["wqb"], lw["wkv"])
    return pl.pallas_call(
        _inproj_body,
        grid=(B, S // tm),
        in_specs=[pl.BlockSpec((None, tm, ROW_SUBLANES, LANES), lambda b, i: (b, i, 0, 0))]
        + [full(w) for w in weights]
        + [pl.BlockSpec((tm, LANES), lambda b, i: (i, 0))] * 2,
        out_specs=[
            pl.BlockSpec((None, tm, FOURIER_WIDTH), lambda b, i: (b, i, 0)),
            pl.BlockSpec((None, N_HEADS, tm, QK_PAD), lambda b, i: (b, 0, i, 0)),
            pl.BlockSpec((None, N_HEADS, tm, QK_PAD), lambda b, i: (b, 0, i, 0)),
            pl.BlockSpec((None, N_HEADS, tm, V_HEAD_DIM), lambda b, i: (b, 0, i, 0)),
        ],
        out_shape=[
            jax.ShapeDtypeStruct((B, S, FOURIER_WIDTH), BF16),
            jax.ShapeDtypeStruct((B, N_HEADS, S, QK_PAD), BF16),
            jax.ShapeDtypeStruct((B, N_HEADS, S, QK_PAD), BF16),
            jax.ShapeDtypeStruct((B, N_HEADS, S, V_HEAD_DIM), BF16),
        ],
        compiler_params=_cparams(("parallel", "parallel")),
        name="inproj",
    )(x, *weights, cos_t, sin_t)


def _dft1_body(u_ref, w1_ref, tc_ref, ts_ref, z_ref):
    n1 = u_ref.shape[0]
    y = _dot(w1_ref[...], u_ref[...])
    yr, yi = y[:n1], y[n1:]
    tc, ts = tc_ref[...], ts_ref[...]
    z_ref[0] = (yr * tc + yi * ts).astype(BF16)
    z_ref[1] = (yi * tc - yr * ts).astype(BF16)


def _dft2_body(z_ref, w2a_ref, w2b_ref, cbd_ref, sbd_ref, wbd_ref, f_ref, *, norm):
    n2 = z_ref.shape[2]
    for j in range(z_ref.shape[1]):
        v = _dot(w2a_ref[...], z_ref[0, j]) + _dot(w2b_ref[...], z_ref[1, j])
        vr, vi = v[:n2].astype(BF16), v[n2:].astype(BF16)
        g = (_dot(vr, cbd_ref[...]) + _dot(vi, sbd_ref[...])) * norm
        f_ref[j] = _dot(g.astype(BF16), wbd_ref[...]).astype(BF16)


def _dft_tables(S):
    n1, n2 = DFT_N1, S // DFT_N1

    def cs(n, rows, cols):
        ang = 2.0 * np.pi * ((np.outer(rows, cols)) % n) / n
        return np.cos(ang), np.sin(ang)

    c1, s1 = cs(n1, np.arange(n1), np.arange(n1))
    w1 = np.concatenate([c1, -s1], axis=0)
    tc, ts = cs(S, np.arange(n1), np.arange(n2))
    tc = np.repeat(tc, FOURIER_WIDTH, axis=1)
    ts = np.repeat(ts, FOURIER_WIDTH, axis=1)
    c2, s2 = cs(n2, np.arange(n2), np.arange(n2))
    w2a = np.concatenate([c2, -s2], axis=0)
    w2b = np.concatenate([s2, c2], axis=0)
    cg, sg = cs(FOURIER_GROUP_DIM, np.arange(FOURIER_GROUP_DIM), np.arange(FOURIER_GROUP_DIM))
    eye = np.eye(FOURIER_GROUPS)
    return dict(
        w1=jnp.asarray(w1, BF16), tc=jnp.asarray(tc, F32), ts=jnp.asarray(ts, F32),
        w2a=jnp.asarray(w2a, BF16), w2b=jnp.asarray(w2b, BF16),
        cbd=jnp.asarray(np.kron(eye, cg), BF16), sbd=jnp.asarray(np.kron(eye, sg), BF16),
    )


def _fourier(u_f, wbd, tabs):
    B, S, W = u_f.shape
    n1, n2 = DFT_N1, S // DFT_N1
    cols = min(DFT_COLS, n2 * W)
    u2 = u_f.reshape(B, n1, n2 * W)
    z = pl.pallas_call(
        _dft1_body,
        grid=(B, n2 * W // cols),
        in_specs=[
            pl.BlockSpec((None, n1, cols), lambda b, c: (b, 0, c)),
            pl.BlockSpec((2 * n1, n1), lambda b, c: (0, 0)),
            pl.BlockSpec((n1, cols), lambda b, c: (0, c)),
            pl.BlockSpec((n1, cols), lambda b, c: (0, c)),
        ],
        out_specs=pl.BlockSpec((None, 2, n1, cols), lambda b, c: (b, 0, 0, c)),
        out_shape=jax.ShapeDtypeStruct((B, 2, n1, n2 * W), BF16),
        compiler_params=_cparams(("parallel", "parallel")),
        name="dft_stage1",
    )(u2, tabs["w1"], tabs["tc"], tabs["ts"])
    z5 = z.reshape(B, 2, n1, n2, W)
    kb = DFT_KB
    f4 = pl.pallas_call(
        functools.partial(_dft2_body, norm=float((S * FOURIER_GROUP_DIM) ** -0.5)),
        grid=(B, n1 // kb),
        in_specs=[
            pl.BlockSpec((None, 2, kb, n2, W), lambda b, k: (b, 0, k, 0, 0)),
            pl.BlockSpec((2 * n2, n2), lambda b, k: (0, 0)),
            pl.BlockSpec((2 * n2, n2), lambda b, k: (0, 0)),
            pl.BlockSpec((W, W), lambda b, k: (0, 0)),
            pl.BlockSpec((W, W), lambda b, k: (0, 0)),
            pl.BlockSpec((W, W), lambda b, k: (0, 0)),
        ],
        out_specs=pl.BlockSpec((None, kb, n2, W), lambda b, k: (b, k, 0, 0)),
        out_shape=jax.ShapeDtypeStruct((B, n1, n2, W), BF16),
        compiler_params=_cparams(("parallel", "parallel")),
        name="dft_stage2",
    )(z5, tabs["w2a"], tabs["w2b"], tabs["cbd"], tabs["sbd"], wbd)
    return jnp.transpose(f4, (0, 2, 1, 3)).reshape(B, S, W)


def _attn_body(q_ref, k_ref, v_ref, o_ref, *, tk):
    q = q_ref[...]
    tq = q.shape[0]
    n_kv = k_ref.shape[0] // tk

    def step(j, carry):
        m, l, acc = carry
        off = pl.multiple_of(j * tk, tk)
        kj = k_ref[pl.ds(off, tk), :]
        vj = v_ref[pl.ds(off, tk), :]
        s = lax.dot_general(q, kj, (((1,), (1,)), ((), ())), preferred_element_type=F32)
        m_new = jnp.maximum(m, jnp.max(s, axis=-1, keepdims=True))
        alpha = jnp.exp2(m - m_new)
        p = jnp.exp2(s - m_new)
        l = alpha * l + jnp.sum(p, axis=-1, keepdims=True)
        acc = alpha * acc + _dot(p.astype(BF16), vj)
        return m_new, l, acc

    init = (jnp.full((tq, 1), -jnp.inf, F32), jnp.zeros((tq, 1), F32),
            jnp.zeros((tq, V_HEAD_DIM), F32))
    _, l, acc = lax.fori_loop(0, n_kv, step, init, unroll=math.gcd(KV_UNROLL, n_kv))
    o_ref[...] = (acc / l).astype(BF16)


def _attention(q, k, v):
    B, H, S, _ = q.shape
    tq, tk = min(TQ, S), min(TK, S)
    return pl.pallas_call(
        functools.partial(_attn_body, tk=tk),
        grid=(B, H, S // tq),
        in_specs=[
            pl.BlockSpec((None, None, tq, QK_PAD), lambda b, h, i: (b, h, i, 0)),
            pl.BlockSpec((None, None, S, QK_PAD), lambda b, h, i: (b, h, 0, 0)),
            pl.BlockSpec((None, None, S, V_HEAD_DIM), lambda b, h, i: (b, h, 0, 0)),
        ],
        out_specs=pl.BlockSpec((None, tq, V_HEAD_DIM), lambda b, h, i: (b, i, h)),
        out_shape=jax.ShapeDtypeStruct((B, S, H * V_HEAD_DIM), BF16),
        compiler_params=_cparams(("parallel", "parallel", "parallel")),
        name="attention",
    )(q, k, v)


def _layernorm(y, g, b):
    mu = jnp.mean(y, axis=-1, keepdims=True)
    d = y - mu
    var = jnp.mean(d * d, axis=-1, keepdims=True)
    return d * lax.rsqrt(var + LN_EPS) * g + b


def _sigmoid(x):
    return 1.0 / (1.0 + jnp.exp(-x))


def _outproj_body(x_ref, f_ref, a_ref, wf_ref, wa_ref, g_ref, b_ref, wra_ref, wrb_ref, bias_ref,
                  tril_ref, x1_ref, meta_ref, cnt_ref, run_ref):
    @pl.when(pl.program_id(0) == 0)
    def _():
        run_ref[...] = jnp.zeros_like(run_ref)

    mix = _dot(f_ref[...], wf_ref[...]) + _dot(a_ref[...], wa_ref[...])
    x1 = _layernorm(DEEPNORM_ALPHA * _load_rows(x_ref) + mix, g_ref[...], b_ref[...])
    _store_rows(x1_ref, x1)
    xb = x1.astype(BF16)
    pair_sum = _sigmoid(_dot(xb, wra_ref[...])) + _sigmoid(_dot(xb, wrb_ref[...])) + bias_ref[...]
    lane = lax.broadcasted_iota(jnp.int32, pair_sum.shape, 1)
    best = jnp.max(pair_sum, axis=-1, keepdims=True)
    cls = jnp.min(jnp.where(pair_sum == best, lane, LANES), axis=-1, keepdims=True)
    onehot = lane == cls
    prefix = _dot(tril_ref[...], onehot.astype(BF16))
    run = run_ref[...]
    rank = jnp.sum(jnp.where(onehot, prefix + run, 0.0), axis=-1, keepdims=True) - 1.0
    meta_ref[...] = jnp.where(lane == 0, cls, jnp.where(lane == 1, rank.astype(jnp.int32), 0))
    run = run + prefix[prefix.shape[0] - 1:, :]
    run_ref[...] = run
    cnt_ref[...] = run


def _outproj(x3d, f2d, a2d, lw, rw):
    T = x3d.shape[0]
    tm = min(TM_TOKEN, T)
    tril = jnp.asarray(np.tril(np.ones((tm, tm), np.float32)), BF16)
    full = lambda a: pl.BlockSpec(a.shape, lambda i: (0,) * a.ndim)
    row = lambda w: pl.BlockSpec((tm, w), lambda i: (i, 0))
    tok = pl.BlockSpec((tm, ROW_SUBLANES, LANES), lambda i: (i, 0, 0))
    consts = (lw["w_out_f"], lw["w_out_a"], lw["ln1_g"], lw["ln1_b"], rw["wra"], rw["wrb"],
              rw["bias_ab"], tril)
    return pl.pallas_call(
        _outproj_body,
        grid=(T // tm,),
        in_specs=[tok, row(FOURIER_WIDTH), row(N_HEADS * V_HEAD_DIM)] + [full(c) for c in consts],
        out_specs=[tok, row(LANES), pl.BlockSpec((1, LANES), lambda i: (0, 0))],
        out_shape=[
            jax.ShapeDtypeStruct((T, ROW_SUBLANES, LANES), F32),
            jax.ShapeDtypeStruct((T, LANES), jnp.int32),
            jax.ShapeDtypeStruct((1, LANES), F32),
        ],
        scratch_shapes=[pltpu.VMEM((1, LANES), F32)],
        compiler_params=_cparams(("arbitrary",)),
        name="outproj_route",
    )(x3d, f2d, a2d, *consts)


def _scatter_rows_body(pos_ref, x_ref, init_hbm, dst_hbm, sem, *, chunk):
    del init_hbm

    def copy(j):
        return pltpu.make_async_copy(x_ref.at[j], dst_hbm.at[pos_ref[j]], sem)

    def issue(j, c):
        copy(j).start()
        return c

    def drain(j, c):
        copy(j).wait()
        return c

    lax.fori_loop(0, chunk, issue, 0)
    lax.fori_loop(0, chunk, drain, 0, unroll=DRAIN_UNROLL)


def _gather_rows_body(pos_ref, src_hbm, o_ref, sem, *, chunk):
    def copy(j):
        return pltpu.make_async_copy(src_hbm.at[pos_ref[j]], o_ref.at[j], sem)

    def issue(j, c):
        copy(j).start()
        return c

    def drain(j, c):
        copy(j).wait()
        return c

    lax.fori_loop(0, chunk, issue, 0)
    lax.fori_loop(0, chunk, drain, 0, unroll=DRAIN_UNROLL)


def _scatter_rows(src, pos, n_dst):
    T = src.shape[0]
    chunk = min(ROW_CHUNK, T)
    return pl.pallas_call(
        functools.partial(_scatter_rows_body, chunk=chunk),
        grid=(T // chunk,),
        in_specs=[
            pl.BlockSpec((chunk,), lambda i: (i,), memory_space=pltpu.SMEM),
            pl.BlockSpec((chunk,) + src.shape[1:], lambda i: (i, 0, 0)),
            pl.BlockSpec(memory_space=pl.ANY),
        ],
        out_specs=pl.BlockSpec(memory_space=pl.ANY),
        out_shape=jax.ShapeDtypeStruct((n_dst,) + src.shape[1:], src.dtype),
        scratch_shapes=[pltpu.SemaphoreType.DMA(())],
        input_output_aliases={2: 0},
        compiler_params=_cparams(("arbitrary",)),
        name="scatter_rows",
    )(pos, src, jnp.zeros((n_dst,) + src.shape[1:], src.dtype))


def _gather_rows(src, pos):
    T = pos.shape[0]
    chunk = min(ROW_CHUNK, T)
    return pl.pallas_call(
        functools.partial(_gather_rows_body, chunk=chunk),
        grid=(T // chunk,),
        in_specs=[
            pl.BlockSpec((chunk,), lambda i: (i,), memory_space=pltpu.SMEM),
            pl.BlockSpec(memory_space=pl.ANY),
        ],
        out_specs=pl.BlockSpec((chunk,) + src.shape[1:], lambda i: (i, 0, 0)),
        out_shape=jax.ShapeDtypeStruct((T,) + src.shape[1:], src.dtype),
        scratch_shapes=[pltpu.SemaphoreType.DMA(())],
        compiler_params=_cparams(("arbitrary",)),
        name="gather_rows",
    )(pos, src)


def _swiglu(xb, wg, wu, wd):
    g = _dot(xb, wg)
    h = (g * _sigmoid(g)) * _dot(xb, wu)
    return _dot(h.astype(BF16), wd)


def _moe_body(ea_ref, eb_ref, nused_ref, x_ref, wr_ref, wsg_ref, wsu_ref, wsd_ref,
              wga_ref, wua_ref, wda_ref, wgb_ref, wub_ref, wdb_ref, g_ref, b_ref, y_ref):
    i = pl.program_id(0)

    @pl.when(i < nused_ref[0])
    def _():
        x = _load_rows(x_ref)
        xb = x.astype(BF16)
        scores = _sigmoid(_dot(xb, wr_ref[...]))
        lane = lax.broadcasted_iota(jnp.int32, scores.shape, 1)
        sa = jnp.sum(jnp.where(lane == ea_ref[i], scores, 0.0), axis=-1, keepdims=True)
        sb = jnp.sum(jnp.where(lane == eb_ref[i], scores, 0.0), axis=-1, keepdims=True)
        inv = 1.0 / (sa + sb)
        y = _swiglu(xb, wsg_ref[...], wsu_ref[...], wsd_ref[...])
        y = y + (sa * inv) * _swiglu(xb, wga_ref[...], wua_ref[...], wda_ref[...])
        y = y + (sb * inv) * _swiglu(xb, wgb_ref[...], wub_ref[...], wdb_ref[...])
        _store_rows(y_ref, _layernorm(DEEPNORM_ALPHA * x + y, g_ref[...], b_ref[...]))

    @pl.when(i >= nused_ref[0])
    def _():
        y_ref[...] = jnp.zeros_like(y_ref)


def _moe(xs, ea, eb, nused, lw, rw):
    n_rows = xs.shape[0]
    tm = TM_MOE
    full = lambda a: pl.BlockSpec(a.shape, lambda i, ea, eb, nu: (0,) * a.ndim)
    exp_a = lambda a: pl.BlockSpec((None,) + a.shape[1:], lambda i, ea, eb, nu: (ea[i], 0, 0))
    exp_b = lambda a: pl.BlockSpec((None,) + a.shape[1:], lambda i, ea, eb, nu: (eb[i], 0, 0))
    row = pl.BlockSpec((tm, ROW_SUBLANES, LANES), lambda i, ea, eb, nu: (i, 0, 0))
    shared = (rw["wr_pad"], lw["ws_gate"], lw["ws_up"], lw["ws_down"])
    routed = (lw["w_gate"], lw["w_up"], lw["w_down"])
    grid_spec = pltpu.PrefetchScalarGridSpec(
        num_scalar_prefetch=3,
        grid=(n_rows // tm,),
        in_specs=[row] + [full(w) for w in shared] + [exp_a(w) for w in routed]
        + [exp_b(w) for w in routed] + [full(lw["ln2_g"]), full(lw["ln2_b"])],
        out_specs=row,
    )
    return pl.pallas_call(
        _moe_body,
        grid_spec=grid_spec,
        out_shape=jax.ShapeDtypeStruct(xs.shape, F32),
        compiler_params=_cparams(("arbitrary",)),
        name="moe_ln2",
    )(ea, eb, nused, xs, *shared, *routed, *routed, lw["ln2_g"], lw["ln2_b"])


def _rotate_half_cols(w):
    half = w.shape[1] // 2
    return jnp.concatenate([-w[:, half:], w[:, :half]], axis=1)


def _prep_layer(i, p):
    w_in = p["w_in"][i]
    k_r_lo = FOURIER_WIDTH + Q_LORA_RANK + KV_LORA_RANK
    w_in_ext = jnp.concatenate([w_in, _rotate_half_cols(w_in[:, k_r_lo:])], axis=1)
    wq = p["w_q_up"][i].reshape(Q_LORA_RANK, N_HEADS, QK_NOPE_DIM + QK_ROPE_DIM)
    zeros = jnp.zeros((Q_LORA_RANK, N_HEADS, QK_PAD - QK_NOPE_DIM - QK_ROPE_DIM), F32)
    wqa = jnp.concatenate([wq, zeros], axis=2).reshape(Q_LORA_RANK, N_HEADS * QK_PAD)
    rot = jnp.concatenate([-wq[:, :, QK_NOPE_DIM + QK_ROPE_DIM // 2:],
                           wq[:, :, QK_NOPE_DIM:QK_NOPE_DIM + QK_ROPE_DIM // 2], zeros], axis=2)
    wqb = rot.reshape(Q_LORA_RANK, N_HEADS * LANES)
    eye = jnp.eye(FOURIER_GROUPS, dtype=F32)
    wbd = (eye[:, None, :, None] * p["w_fourier"][i][:, :, None, :]).reshape(FOURIER_WIDTH, FOURIER_WIDTH)
    row = lambda v: v[i].reshape(1, -1).astype(F32)
    return dict(
        w_in=w_in_ext.astype(BF16), q_g=row(p["q_norm_g"]), kv_g=row(p["kv_norm_g"]),
        wqa=wqa.astype(BF16), wqb=wqb.astype(BF16), wkv=p["w_kv_up"][i].astype(BF16),
        wbd=wbd.astype(BF16),
        w_out_f=p["w_out"][i][:FOURIER_WIDTH].astype(BF16),
        w_out_a=p["w_out"][i][FOURIER_WIDTH:].astype(BF16),
        ln1_g=row(p["ln1_g"]), ln1_b=row(p["ln1_b"]), ln2_g=row(p["ln2_g"]), ln2_b=row(p["ln2_b"]),
        w_gate=p["w_gate"][i].astype(BF16), w_up=p["w_up"][i].astype(BF16),
        w_down=p["w_down"][i].astype(BF16),
        ws_gate=p["ws_gate"][i].astype(BF16), ws_up=p["ws_up"][i].astype(BF16),
        ws_down=p["ws_down"][i].astype(BF16),
    )


def _class_members():
    a = np.array([EXPERTS_PER_GROUP * g + PAIRS[q][0] for g in range(N_GROUPS) for q in range(N_PAIRS)])
    b = np.array([EXPERTS_PER_GROUP * g + PAIRS[q][1] for g in range(N_GROUPS) for q in range(N_PAIRS)])
    return a, b


def _prep_router(w_router, router_bias):
    a, b = _class_members()
    pad = jnp.zeros((D_MODEL, LANES - N_CLASSES), F32)
    wra = jnp.concatenate([w_router[:, a], pad], axis=1)
    wrb = jnp.concatenate([w_router[:, b], pad], axis=1)
    bias = router_bias.astype(F32)
    bias_ab = jnp.concatenate([bias[a] + bias[b], jnp.full((LANES - N_CLASSES,), -jnp.inf, F32)])
    wr_pad = jnp.concatenate([w_router, jnp.zeros((D_MODEL, LANES - N_EXPERTS), F32)], axis=1)
    return dict(wra=wra.astype(BF16), wrb=wrb.astype(BF16), bias_ab=bias_ab.reshape(1, LANES),
                wr_pad=wr_pad.astype(BF16))


def _rope_tables(S):
    pos = jnp.arange(S, dtype=F32)
    inv_freq = ROPE_THETA ** (-jnp.arange(0, QK_ROPE_DIM, 2, dtype=F32) / QK_ROPE_DIM)
    ang = pos[:, None] * inv_freq[None, :]
    zeros = jnp.zeros((S, LANES - QK_ROPE_DIM), F32)
    cos_t = jnp.concatenate([jnp.cos(ang), jnp.cos(ang), zeros], axis=1)
    sin_t = jnp.concatenate([jnp.sin(ang), jnp.sin(ang), zeros], axis=1)
    return cos_t, sin_t


def _dispatch_plan(meta, counts, T):
    tm = TM_MOE
    cls, rank = meta[:, 0], meta[:, 1]
    cnt = counts[0, :N_CLASSES].astype(jnp.int32)
    padded = ((cnt + tm - 1) // tm) * tm
    ends = jnp.cumsum(padded)
    starts = ends - padded
    pos = starts[cls] + rank
    n_tiles = T // tm + N_CLASSES
    nused = ends[-1] // tm
    tile_row = jnp.minimum(jnp.arange(n_tiles, dtype=jnp.int32), nused - 1) * tm
    tile_cls = jnp.sum((ends[None, :] <= tile_row[:, None]).astype(jnp.int32), axis=1)
    tile_cls = jnp.minimum(tile_cls, N_CLASSES - 1)
    a, b = _class_members()
    ea = jnp.asarray(a, jnp.int32)[tile_cls]
    eb = jnp.asarray(b, jnp.int32)[tile_cls]
    return pos.astype(jnp.int32), ea, eb, nused.reshape(1).astype(jnp.int32), n_tiles * tm


def _layer(x, lw, rw, rope, tabs):
    B, S = x.shape[:2]
    T = B * S
    u_f, q, k, v = _inproj(x, lw, *rope)
    f = _fourier(u_f, lw["wbd"], tabs)
    a = _attention(q, k, v)
    x1, meta, counts = _outproj(x.reshape((T,) + x.shape[2:]), f.reshape(T, -1), a.reshape(T, -1),
                                lw, rw)
    pos, ea, eb, nused, n_rows = _dispatch_plan(meta, counts, T)
    xs = _scatter_rows(x1, pos, n_rows)
    ys = _moe(xs, ea, eb, nused, lw, rw)
    return _gather_rows(ys, pos).reshape(x.shape)


def _trunk(x, layers, rw):
    B, S, D = x.shape
    rope = _rope_tables(S)
    tabs = _dft_tables(S)
    x = x.reshape(B, S, ROW_SUBLANES, LANES)
    for lw in layers:
        x = _layer(x, lw, rw, rope, tabs)
    return x.reshape(B, S, D)


def kernel(x_prompt, x_sample, w_in, q_norm_g, w_q_up, kv_norm_g, w_kv_up, w_fourier, w_out,
           ln1_g, ln1_b, w_router, router_bias, w_gate, w_up, w_down, ws_gate, ws_up, ws_down,
           ln2_g, ln2_b):
    p = dict(w_in=w_in, q_norm_g=q_norm_g, w_q_up=w_q_up, kv_norm_g=kv_norm_g, w_kv_up=w_kv_up,
             w_fourier=w_fourier, w_out=w_out, ln1_g=ln1_g, ln1_b=ln1_b, w_gate=w_gate, w_up=w_up,
             w_down=w_down, ws_gate=ws_gate, ws_up=ws_up, ws_down=ws_down, ln2_g=ln2_g, ln2_b=ln2_b)
    layers = [_prep_layer(i, p) for i in range(w_in.shape[0])]
    rw = _prep_router(w_router, router_bias)
    return _trunk(x_prompt, layers, rw), _trunk(x_sample, layers, rw)
```

```python
import functools
import math

import numpy as np
import jax
import jax.numpy as jnp
from jax import lax
from jax.experimental import pallas as pl
from jax.experimental.pallas import tpu as pltpu

F32 = jnp.float32
BF16 = jnp.bfloat16

D_MODEL = 1024
DEPTH = 2
FOURIER_WIDTH = 256
FOURIER_GROUPS = 4
FOURIER_GROUP_DIM = 64
N_HEADS = 6
QK_NOPE_DIM = 128
QK_ROPE_DIM = 64
V_HEAD_DIM = 128
Q_LORA_RANK = 384
KV_LORA_RANK = 256
ROPE_THETA = 10000.0
N_EXPERTS = 16
N_GROUPS = 4
EXPERTS_PER_GROUP = 4
D_FF = 512
DEEPNORM_ALPHA = (2 * DEPTH) ** 0.25
LN_EPS = 1e-5
RMS_EPS = 1e-6

LANES = 128
QK_PAD = 256
N_PAIRS = 6
N_CLASSES = N_GROUPS * N_PAIRS
PAIRS = ((0, 1), (0, 2), (0, 3), (1, 2), (1, 3), (2, 3))

TM_TOKEN = 512
TM_MOE = 512
META_ROWS = 8
TQ = 1024
TK = 512
KV_UNROLL = 8
SUBLANES = 8
DFT_N1 = 64
DFT_COLS = 4096
DFT_KB = 8
ROW_CHUNK = 1024
VMEM_LIMIT = 48 * 1024 * 1024

SOFTMAX_SCALE = (QK_NOPE_DIM + QK_ROPE_DIM) ** -0.5
LOG2E = math.log2(math.e)


def _cparams(sem):
    return pltpu.CompilerParams(dimension_semantics=sem, vmem_limit_bytes=VMEM_LIMIT)


def _dot(a, b):
    return jnp.dot(a, b, preferred_element_type=F32)


def _tiled(n_tokens):
    return (n_tokens // SUBLANES, D_MODEL // LANES, SUBLANES, LANES)


def _load_rows(ref):
    if len(ref.shape) == 2:
        return ref[...]
    rows = ref.shape[0] * SUBLANES
    return jnp.concatenate([ref[:, c].reshape(rows, LANES) for c in range(ref.shape[1])], axis=1)


def _store_rows(ref, val):
    for c in range(ref.shape[1]):
        ref[:, c] = val[:, c * LANES:(c + 1) * LANES].reshape(ref.shape[0], SUBLANES, LANES)


def _batch_seq(x):
    return (x.shape[0], x.shape[1]) if x.ndim == 3 else (x.shape[0], x.shape[1] * SUBLANES)


def _token_tile(ref, j):
    return ref.at[lax.shift_right_logical(j, 3), :, jnp.bitwise_and(j, SUBLANES - 1), :]


def _inproj_body(x_ref, win_ref, qg_ref, kvg_ref, wqa_ref, wqb_ref, wkv_ref, cos_ref, sin_ref,
                 uf_ref, q_ref, k_ref, v_ref):
    x = _load_rows(x_ref)
    h = _dot(x.astype(BF16), win_ref[...])
    uf_ref[...] = h[:, :FOURIER_WIDTH].astype(BF16)

    def rms(c, g):
        return (c * lax.rsqrt(jnp.mean(c * c, axis=-1, keepdims=True) + RMS_EPS) * g).astype(BF16)

    c_q = rms(h[:, FOURIER_WIDTH:FOURIER_WIDTH + Q_LORA_RANK], qg_ref[...])
    kv_lo = FOURIER_WIDTH + Q_LORA_RANK
    c_kv = rms(h[:, kv_lo:kv_lo + KV_LORA_RANK], kvg_ref[...])
    cos = cos_ref[...]
    sin = sin_ref[...]
    t = h[:, kv_lo + KV_LORA_RANK:]
    k_rope = (t * cos + pltpu.roll(t, 64, 1) * sin).astype(BF16)
    qa = _dot(c_q, wqa_ref[...])
    qb = _dot(c_q, wqb_ref[...])
    kv = _dot(c_kv, wkv_ref[...])
    qs = SOFTMAX_SCALE * LOG2E
    for hd in range(N_HEADS):
        lo = hd * QK_PAD
        q_ref[hd, :, :LANES] = (qa[:, lo:lo + LANES] * qs).astype(BF16)
        q_rope = qa[:, lo + LANES:lo + QK_PAD] * cos + qb[:, hd * LANES:(hd + 1) * LANES] * sin
        q_ref[hd, :, LANES:] = (q_rope * qs).astype(BF16)
        k_ref[hd, :, :LANES] = kv[:, lo:lo + LANES].astype(BF16)
        k_ref[hd, :, LANES:] = k_rope
        v_ref[hd] = kv[:, lo + LANES:lo + QK_PAD].astype(BF16)


def _inproj(x, lw, cos_t, sin_t):
    B, S = _batch_seq(x)
    tm = min(TM_TOKEN, S)
    blk = (tm, D_MODEL) if x.ndim == 3 else _tiled(tm)
    x_spec = pl.BlockSpec((None,) + blk, lambda b, i: (b, i) + (0,) * (len(blk) - 1))
    full = lambda a: pl.BlockSpec(a.shape, lambda b, i: (0,) * a.ndim)
    weights = (lw["w_in"], lw["q_g"], lw["kv_g"], lw["wqa"], lw["wqb"], lw["wkv"])
    return pl.pallas_call(
        _inproj_body,
        grid=(B, S // tm),
        in_specs=[x_spec]
        + [full(w) for w in weights]
        + [pl.BlockSpec((tm, LANES), lambda b, i: (i, 0))] * 2,
        out_specs=[
            pl.BlockSpec((None, tm, FOURIER_WIDTH), lambda b, i: (b, i, 0)),
            pl.BlockSpec((None, N_HEADS, tm, QK_PAD), lambda b, i: (b, 0, i, 0)),
            pl.BlockSpec((None, N_HEADS, tm, QK_PAD), lambda b, i: (b, 0, i, 0)),
            pl.BlockSpec((None, N_HEADS, tm, V_HEAD_DIM), lambda b, i: (b, 0, i, 0)),
        ],
        out_shape=[
            jax.ShapeDtypeStruct((B, S, FOURIER_WIDTH), BF16),
            jax.ShapeDtypeStruct((B, N_HEADS, S, QK_PAD), BF16),
            jax.ShapeDtypeStruct((B, N_HEADS, S, QK_PAD), BF16),
            jax.ShapeDtypeStruct((B, N_HEADS, S, V_HEAD_DIM), BF16),
        ],
        compiler_params=_cparams(("parallel", "parallel")),
        name="inproj",
    )(x, *weights, cos_t, sin_t)


def _dft1_body(u_ref, w1_ref, tc_ref, ts_ref, z_ref):
    n1 = u_ref.shape[0]
    y = _dot(w1_ref[...], u_ref[...])
    yr, yi = y[:n1], y[n1:]
    tc, ts = tc_ref[...], ts_ref[...]
    z_ref[0] = (yr * tc + yi * ts).astype(BF16)
    z_ref[1] = (yi * tc - yr * ts).astype(BF16)


def _dft2_body(z_ref, w2a_ref, w2b_ref, cbd_ref, sbd_ref, wbd_ref, f_ref, *, norm):
    n2 = z_ref.shape[2]
    for j in range(z_ref.shape[1]):
        v = _dot(w2a_ref[...], z_ref[0, j]) + _dot(w2b_ref[...], z_ref[1, j])
        vr, vi = v[:n2].astype(BF16), v[n2:].astype(BF16)
        g = (_dot(vr, cbd_ref[...]) + _dot(vi, sbd_ref[...])) * norm
        f_ref[j] = _dot(g.astype(BF16), wbd_ref[...]).astype(BF16)


def _dft_tables(S):
    n1, n2 = DFT_N1, S // DFT_N1

    def cs(n, rows, cols):
        ang = 2.0 * np.pi * ((np.outer(rows, cols)) % n) / n
        return np.cos(ang), np.sin(ang)

    c1, s1 = cs(n1, np.arange(n1), np.arange(n1))
    w1 = np.concatenate([c1, -s1], axis=0)
    tc, ts = cs(S, np.arange(n1), np.arange(n2))
    tc = np.repeat(tc, FOURIER_WIDTH, axis=1)
    ts = np.repeat(ts, FOURIER_WIDTH, axis=1)
    c2, s2 = cs(n2, np.arange(n2), np.arange(n2))
    w2a = np.concatenate([c2, -s2], axis=0)
    w2b = np.concatenate([s2, c2], axis=0)
    cg, sg = cs(FOURIER_GROUP_DIM, np.arange(FOURIER_GROUP_DIM), np.arange(FOURIER_GROUP_DIM))
    eye = np.eye(FOURIER_GROUPS)
    return dict(
        w1=jnp.asarray(w1, BF16), tc=jnp.asarray(tc, F32), ts=jnp.asarray(ts, F32),
        w2a=jnp.asarray(w2a, BF16), w2b=jnp.asarray(w2b, BF16),
        cbd=jnp.asarray(np.kron(eye, cg), BF16), sbd=jnp.asarray(np.kron(eye, sg), BF16),
    )


def _fourier(u_f, wbd, tabs):
    B, S, W = u_f.shape
    n1, n2 = DFT_N1, S // DFT_N1
    cols = min(DFT_COLS, n2 * W)
    u2 = u_f.reshape(B, n1, n2 * W)
    z = pl.pallas_call(
        _dft1_body,
        grid=(B, n2 * W // cols),
        in_specs=[
            pl.BlockSpec((None, n1, cols), lambda b, c: (b, 0, c)),
            pl.BlockSpec((2 * n1, n1), lambda b, c: (0, 0)),
            pl.BlockSpec((n1, cols), lambda b, c: (0, c)),
            pl.BlockSpec((n1, cols), lambda b, c: (0, c)),
        ],
        out_specs=pl.BlockSpec((None, 2, n1, cols), lambda b, c: (b, 0, 0, c)),
        out_shape=jax.ShapeDtypeStruct((B, 2, n1, n2 * W), BF16),
        compiler_params=_cparams(("parallel", "parallel")),
        name="dft_stage1",
    )(u2, tabs["w1"], tabs["tc"], tabs["ts"])
    z5 = z.reshape(B, 2, n1, n2, W)
    kb = DFT_KB
    f4 = pl.pallas_call(
        functools.partial(_dft2_body, norm=float((S * FOURIER_GROUP_DIM) ** -0.5)),
        grid=(B, n1 // kb),
        in_specs=[
            pl.BlockSpec((None, 2, kb, n2, W), lambda b, k: (b, 0, k, 0, 0)),
            pl.BlockSpec((2 * n2, n2), lambda b, k: (0, 0)),
            pl.BlockSpec((2 * n2, n2), lambda b, k: (0, 0)),
            pl.BlockSpec((W, W), lambda b, k: (0, 0)),
            pl.BlockSpec((W, W), lambda b, k: (0, 0)),
            pl.BlockSpec((W, W), lambda b, k: (0, 0)),
        ],
        out_specs=pl.BlockSpec((None, kb, n2, W), lambda b, k: (b, k, 0, 0)),
        out_shape=jax.ShapeDtypeStruct((B, n1, n2, W), BF16),
        compiler_params=_cparams(("parallel", "parallel")),
        name="dft_stage2",
    )(z5, tabs["w2a"], tabs["w2b"], tabs["cbd"], tabs["sbd"], wbd)
    return jnp.transpose(f4, (0, 2, 1, 3)).reshape(B, S, W)


def _attn_body(q_ref, k_ref, v_ref, o_ref, *, tk):
    q = q_ref[...]
    tq = q.shape[0]
    n_kv = k_ref.shape[0] // tk

    def step(j, carry):
        m, l, acc = carry
        off = pl.multiple_of(j * tk, tk)
        kj = k_ref[pl.ds(off, tk), :]
        vj = v_ref[pl.ds(off, tk), :]
        s = lax.dot_general(q, kj, (((1,), (1,)), ((), ())), preferred_element_type=F32)
        m_new = jnp.maximum(m, jnp.max(s, axis=-1, keepdims=True))
        alpha = jnp.exp2(m - m_new)
        p = jnp.exp2(s - m_new)
        l = alpha * l + jnp.sum(p, axis=-1, keepdims=True)
        acc = alpha * acc + _dot(p.astype(BF16), vj)
        return m_new, l, acc

    init = (jnp.full((tq, 1), -jnp.inf, F32), jnp.zeros((tq, 1), F32),
            jnp.zeros((tq, V_HEAD_DIM), F32))
    _, l, acc = lax.fori_loop(0, n_kv, step, init, unroll=math.gcd(KV_UNROLL, n_kv))
    o_ref[...] = (acc / l).astype(BF16)


def _attention(q, k, v):
    B, H, S, _ = q.shape
    tq, tk = min(TQ, S), min(TK, S)
    return pl.pallas_call(
        functools.partial(_attn_body, tk=tk),
        grid=(B, H, S // tq),
        in_specs=[
            pl.BlockSpec((None, None, tq, QK_PAD), lambda b, h, i: (b, h, i, 0)),
            pl.BlockSpec((None, None, S, QK_PAD), lambda b, h, i: (b, h, 0, 0)),
            pl.BlockSpec((None, None, S, V_HEAD_DIM), lambda b, h, i: (b, h, 0, 0)),
        ],
        out_specs=pl.BlockSpec((None, tq, V_HEAD_DIM), lambda b, h, i: (b, i, h)),
        out_shape=jax.ShapeDtypeStruct((B, S, H * V_HEAD_DIM), BF16),
        compiler_params=_cparams(("parallel", "parallel", "parallel")),
        name="attention",
    )(q, k, v)


def _layernorm(y, g, b):
    mu = jnp.mean(y, axis=-1, keepdims=True)
    d = y - mu
    var = jnp.mean(d * d, axis=-1, keepdims=True)
    return d * lax.rsqrt(var + LN_EPS) * g + b


def _sigmoid(x):
    return 1.0 / (1.0 + jnp.exp(-x))


def _outproj_body(x_ref, f_ref, a_ref, wf_ref, wa_ref, g_ref, b_ref, wra_ref, wrb_ref, bias_ref,
                  tril_ref, run0_ref, x1_ref, meta_ref, cnt_ref, run_ref):
    @pl.when(pl.program_id(0) == 0)
    def _():
        run_ref[...] = run0_ref[...]

    mix = _dot(f_ref[...], wf_ref[...]) + _dot(a_ref[...], wa_ref[...])
    x1 = _layernorm(DEEPNORM_ALPHA * _load_rows(x_ref) + mix, g_ref[...], b_ref[...])
    _store_rows(x1_ref, x1)
    xb = x1.astype(BF16)
    pair_sum = _sigmoid(_dot(xb, wra_ref[...])) + _sigmoid(_dot(xb, wrb_ref[...])) + bias_ref[...]
    lane = lax.broadcasted_iota(jnp.int32, pair_sum.shape, 1)
    best = jnp.max(pair_sum, axis=-1, keepdims=True)
    cls = jnp.min(jnp.where(pair_sum == best, lane, LANES), axis=-1, keepdims=True)
    onehot = lane == cls
    prefix = _dot(tril_ref[...], onehot.astype(BF16))
    run = run_ref[...]
    rank = jnp.sum(jnp.where(onehot, prefix + run, 0.0), axis=-1, keepdims=True) - 1.0
    meta = jnp.where(lane == 0, cls.astype(F32), jnp.where(lane == 1, rank, 0.0))
    meta_ref[...] = jnp.transpose(meta)[:META_ROWS].astype(jnp.int32)
    run = run + prefix[prefix.shape[0] - 1:, :]
    run_ref[...] = run
    cnt_ref[...] = run


def _outproj(x_tok, f2d, a2d, lw, rw, run0):
    T = f2d.shape[0]
    tm = min(TM_TOKEN, T)
    tril = jnp.asarray(np.tril(np.ones((tm, tm), np.float32)), BF16)
    full = lambda a: pl.BlockSpec(a.shape, lambda i: (0,) * a.ndim)
    row = lambda w: pl.BlockSpec((tm, w), lambda i: (i, 0))
    tok = pl.BlockSpec(_tiled(tm), lambda i: (i, 0, 0, 0))
    x_spec = row(D_MODEL) if x_tok.ndim == 2 else tok
    consts = (lw["w_out_f"], lw["w_out_a"], lw["ln1_g"], lw["ln1_b"], rw["wra"], rw["wrb"],
              rw["bias_ab"], tril, run0)
    return pl.pallas_call(
        _outproj_body,
        grid=(T // tm,),
        in_specs=[x_spec, row(FOURIER_WIDTH), row(N_HEADS * V_HEAD_DIM)] + [full(c) for c in consts],
        out_specs=[tok, pl.BlockSpec((META_ROWS, tm), lambda i: (0, i)),
                   pl.BlockSpec((1, LANES), lambda i: (0, 0))],
        out_shape=[
            jax.ShapeDtypeStruct(_tiled(T), F32),
            jax.ShapeDtypeStruct((META_ROWS, T), jnp.int32),
            jax.ShapeDtypeStruct((1, LANES), F32),
        ],
        scratch_shapes=[pltpu.VMEM((1, LANES), F32)],
        compiler_params=_cparams(("arbitrary",)),
        name="outproj_route",
    )(x_tok, f2d, a2d, *consts)


def _scatter_rows_body(pos_ref, x_ref, init_hbm, dst_hbm, sem, *, chunk):
    del init_hbm

    def copy(r, s):
        slot = pos_ref[r * SUBLANES + s]
        return pltpu.make_async_copy(x_ref.at[r, :, s, :], _token_tile(dst_hbm, slot), sem)

    _for_each_token(chunk, copy)


def _gather_rows_body(pos_ref, src_hbm, o_ref, sem, *, chunk):
    def copy(r, s):
        slot = pos_ref[r * SUBLANES + s]
        return pltpu.make_async_copy(_token_tile(src_hbm, slot), o_ref.at[r, :, s, :], sem)

    _for_each_token(chunk, copy)


def _for_each_token(chunk, copy):
    def issue(r, c):
        for s in range(SUBLANES):
            copy(r, s).start()
        return c

    def drain(r, c):
        for s in range(SUBLANES):
            copy(r, s).wait()
        return c

    lax.fori_loop(0, chunk // SUBLANES, issue, 0)
    lax.fori_loop(0, chunk // SUBLANES, drain, 0)


def _scatter_rows(src, pos, dst):
    T = pos.shape[0]
    chunk = min(ROW_CHUNK, T)
    return pl.pallas_call(
        functools.partial(_scatter_rows_body, chunk=chunk),
        grid=(T // chunk,),
        in_specs=[
            pl.BlockSpec((chunk,), lambda i: (i,), memory_space=pltpu.SMEM),
            pl.BlockSpec(_tiled(chunk), lambda i: (i, 0, 0, 0)),
            pl.BlockSpec(memory_space=pl.ANY),
        ],
        out_specs=pl.BlockSpec(memory_space=pl.ANY),
        out_shape=jax.ShapeDtypeStruct(dst.shape, dst.dtype),
        scratch_shapes=[pltpu.SemaphoreType.DMA(())],
        input_output_aliases={2: 0},
        compiler_params=_cparams(("arbitrary",)),
        name="scatter_rows",
    )(pos, src, dst)


def _gather_rows(src, pos):
    T = pos.shape[0]
    chunk = min(ROW_CHUNK, T)
    return pl.pallas_call(
        functools.partial(_gather_rows_body, chunk=chunk),
        grid=(T // chunk,),
        in_specs=[
            pl.BlockSpec((chunk,), lambda i: (i,), memory_space=pltpu.SMEM),
            pl.BlockSpec(memory_space=pl.ANY),
        ],
        out_specs=pl.BlockSpec(_tiled(chunk), lambda i: (i, 0, 0, 0)),
        out_shape=jax.ShapeDtypeStruct(_tiled(T), src.dtype),
        scratch_shapes=[pltpu.SemaphoreType.DMA(())],
        compiler_params=_cparams(("arbitrary",)),
        name="gather_rows",
    )(pos, src)


def _swiglu(xb, wg, wu, wd):
    g = _dot(xb, wg)
    h = (g * _sigmoid(g)) * _dot(xb, wu)
    return _dot(h.astype(BF16), wd)


def _moe_body(ea_ref, eb_ref, nused_ref, x_ref, wr_ref, wsg_ref, wsu_ref, wsd_ref,
              wga_ref, wua_ref, wda_ref, wgb_ref, wub_ref, wdb_ref, g_ref, b_ref, y_ref):
    i = pl.program_id(0)

    @pl.when(i < nused_ref[0])
    def _():
        x = _load_rows(x_ref)
        xb = x.astype(BF16)
        scores = _sigmoid(_dot(xb, wr_ref[...]))
        lane = lax.broadcasted_iota(jnp.int32, scores.shape, 1)
        sa = jnp.sum(jnp.where(lane == ea_ref[i], scores, 0.0), axis=-1, keepdims=True)
        sb = jnp.sum(jnp.where(lane == eb_ref[i], scores, 0.0), axis=-1, keepdims=True)
        inv = 1.0 / (sa + sb)
        y = _swiglu(xb, wsg_ref[...], wsu_ref[...], wsd_ref[...])
        y = y + (sa * inv) * _swiglu(xb, wga_ref[...], wua_ref[...], wda_ref[...])
        y = y + (sb * inv) * _swiglu(xb, wgb_ref[...], wub_ref[...], wdb_ref[...])
        _store_rows(y_ref, _layernorm(DEEPNORM_ALPHA * x + y, g_ref[...], b_ref[...]))

    @pl.when(i >= nused_ref[0])
    def _():
        y_ref[...] = jnp.zeros_like(y_ref)


def _moe(xs, ea, eb, nused, lw, rw):
    n_rows = xs.shape[0] * SUBLANES
    tm = TM_MOE
    full = lambda a: pl.BlockSpec(a.shape, lambda i, ea, eb, nu: (0,) * a.ndim)
    exp_a = lambda a: pl.BlockSpec((None,) + a.shape[1:], lambda i, ea, eb, nu: (ea[i], 0, 0))
    exp_b = lambda a: pl.BlockSpec((None,) + a.shape[1:], lambda i, ea, eb, nu: (eb[i], 0, 0))
    row = pl.BlockSpec(_tiled(tm), lambda i, ea, eb, nu: (i, 0, 0, 0))
    shared = (rw["wr_pad"], lw["ws_gate"], lw["ws_up"], lw["ws_down"])
    routed = (lw["w_gate"], lw["w_up"], lw["w_down"])
    grid_spec = pltpu.PrefetchScalarGridSpec(
        num_scalar_prefetch=3,
        grid=(n_rows // tm,),
        in_specs=[row] + [full(w) for w in shared] + [exp_a(w) for w in routed]
        + [exp_b(w) for w in routed] + [full(lw["ln2_g"]), full(lw["ln2_b"])],
        out_specs=row,
    )
    return pl.pallas_call(
        _moe_body,
        grid_spec=grid_spec,
        out_shape=jax.ShapeDtypeStruct(xs.shape, F32),
        compiler_params=_cparams(("arbitrary",)),
        name="moe_ln2",
    )(ea, eb, nused, xs, *shared, *routed, *routed, lw["ln2_g"], lw["ln2_b"])


def _rotate_half_cols(w):
    half = w.shape[1] // 2
    return jnp.concatenate([-w[:, half:], w[:, :half]], axis=1)


def _prep_layer(i, p):
    w_in = p["w_in"][i]
    k_r_lo = FOURIER_WIDTH + Q_LORA_RANK + KV_LORA_RANK
    w_in_ext = jnp.concatenate([w_in, _rotate_half_cols(w_in[:, k_r_lo:])], axis=1)
    wq = p["w_q_up"][i].reshape(Q_LORA_RANK, N_HEADS, QK_NOPE_DIM + QK_ROPE_DIM)
    zeros = jnp.zeros((Q_LORA_RANK, N_HEADS, QK_PAD - QK_NOPE_DIM - QK_ROPE_DIM), F32)
    wqa = jnp.concatenate([wq, zeros], axis=2).reshape(Q_LORA_RANK, N_HEADS * QK_PAD)
    rot = jnp.concatenate([-wq[:, :, QK_NOPE_DIM + QK_ROPE_DIM // 2:],
                           wq[:, :, QK_NOPE_DIM:QK_NOPE_DIM + QK_ROPE_DIM // 2], zeros], axis=2)
    wqb = rot.reshape(Q_LORA_RANK, N_HEADS * LANES)
    eye = jnp.eye(FOURIER_GROUPS, dtype=F32)
    wbd = (eye[:, None, :, None] * p["w_fourier"][i][:, :, None, :]).reshape(FOURIER_WIDTH, FOURIER_WIDTH)
    row = lambda v: v[i].reshape(1, -1).astype(F32)
    return dict(
        w_in=w_in_ext.astype(BF16), q_g=row(p["q_norm_g"]), kv_g=row(p["kv_norm_g"]),
        wqa=wqa.astype(BF16), wqb=wqb.astype(BF16), wkv=p["w_kv_up"][i].astype(BF16),
        wbd=wbd.astype(BF16),
        w_out_f=p["w_out"][i][:FOURIER_WIDTH].astype(BF16),
        w_out_a=p["w_out"][i][FOURIER_WIDTH:].astype(BF16),
        ln1_g=row(p["ln1_g"]), ln1_b=row(p["ln1_b"]), ln2_g=row(p["ln2_g"]), ln2_b=row(p["ln2_b"]),
        w_gate=p["w_gate"][i].astype(BF16), w_up=p["w_up"][i].astype(BF16),
        w_down=p["w_down"][i].astype(BF16),
        ws_gate=p["ws_gate"][i].astype(BF16), ws_up=p["ws_up"][i].astype(BF16),
        ws_down=p["ws_down"][i].astype(BF16),
    )


def _class_members():
    a = np.array([EXPERTS_PER_GROUP * g + PAIRS[q][0] for g in range(N_GROUPS) for q in range(N_PAIRS)])
    b = np.array([EXPERTS_PER_GROUP * g + PAIRS[q][1] for g in range(N_GROUPS) for q in range(N_PAIRS)])
    return a, b


def _prep_router(w_router, router_bias):
    a, b = _class_members()
    pad = jnp.zeros((D_MODEL, LANES - N_CLASSES), F32)
    wra = jnp.concatenate([w_router[:, a], pad], axis=1)
    wrb = jnp.concatenate([w_router[:, b], pad], axis=1)
    bias = router_bias.astype(F32)
    bias_ab = jnp.concatenate([bias[a] + bias[b], jnp.full((LANES - N_CLASSES,), -jnp.inf, F32)])
    wr_pad = jnp.concatenate([w_router, jnp.zeros((D_MODEL, LANES - N_EXPERTS), F32)], axis=1)
    return dict(wra=wra.astype(BF16), wrb=wrb.astype(BF16), bias_ab=bias_ab.reshape(1, LANES),
                wr_pad=wr_pad.astype(BF16))


def _rope_tables(S):
    pos = jnp.arange(S, dtype=F32)
    inv_freq = ROPE_THETA ** (-jnp.arange(0, QK_ROPE_DIM, 2, dtype=F32) / QK_ROPE_DIM)
    ang = pos[:, None] * inv_freq[None, :]
    zeros = jnp.zeros((S, LANES - QK_ROPE_DIM), F32)
    cos_t = jnp.concatenate([jnp.cos(ang), jnp.cos(ang), zeros], axis=1)
    sin_t = jnp.concatenate([jnp.sin(ang), jnp.sin(ang), zeros], axis=1)
    return cos_t, sin_t


def _dispatch_plan(metas, counts, n_tokens):
    tm = TM_MOE
    cnt = counts[0, :N_CLASSES].astype(jnp.int32)
    padded = ((cnt + tm - 1) // tm) * tm
    ends = jnp.cumsum(padded)
    starts = ends - padded
    pos = [(starts[m[0]] + m[1]).astype(jnp.int32) for m in metas]
    n_tiles = n_tokens // tm + N_CLASSES
    nused = ends[-1] // tm
    tile_row = jnp.minimum(jnp.arange(n_tiles, dtype=jnp.int32), nused - 1) * tm
    tile_cls = jnp.sum((ends[None, :] <= tile_row[:, None]).astype(jnp.int32), axis=1)
    tile_cls = jnp.minimum(tile_cls, N_CLASSES - 1)
    a, b = _class_members()
    ea = jnp.asarray(a, jnp.int32)[tile_cls]
    eb = jnp.asarray(b, jnp.int32)[tile_cls]
    return pos, ea, eb, nused.reshape(1).astype(jnp.int32), n_tiles * tm


def _mix(x, lw, rw, rope, tabs, run0):
    B, S = _batch_seq(x)
    T = B * S
    u_f, q, k, v = _inproj(x, lw, *rope)
    f = _fourier(u_f, lw["wbd"], tabs)
    a = _attention(q, k, v)
    x_tok = x.reshape((T, D_MODEL) if x.ndim == 3 else _tiled(T))
    return _outproj(x_tok, f.reshape(T, -1), a.reshape(T, -1), lw, rw, run0)


def _layer(xs, lw, rw, ropes, tabs):
    run = jnp.zeros((1, LANES), F32)
    x1s, metas = [], []
    for x, rope, tab in zip(xs, ropes, tabs):
        x1, meta, run = _mix(x, lw, rw, rope, tab, run)
        x1s.append(x1)
        metas.append(meta)
    n_tokens = sum(m.shape[1] for m in metas)
    pos, ea, eb, nused, n_rows = _dispatch_plan(metas, run, n_tokens)
    sorted_x = jnp.zeros(_tiled(n_rows), F32)
    for x1, p in zip(x1s, pos):
        sorted_x = _scatter_rows(x1, p, sorted_x)
    sorted_y = _moe(sorted_x, ea, eb, nused, lw, rw)
    out = []
    for x, p in zip(xs, pos):
        B, S = _batch_seq(x)
        out.append(_gather_rows(sorted_y, p).reshape((B,) + _tiled(S)))
    return out


def _untile(x):
    B, S = _batch_seq(x)
    return jnp.transpose(x, (0, 1, 3, 2, 4)).reshape(B, S, D_MODEL)


def _trunks(xs, layers, rw):
    ropes = [_rope_tables(x.shape[1]) for x in xs]
    tabs = [_dft_tables(x.shape[1]) for x in xs]
    for lw in layers:
        xs = _layer(xs, lw, rw, ropes, tabs)
    return tuple(_untile(x) for x in xs)


def kernel(x_prompt, x_sample, w_in, q_norm_g, w_q_up, kv_norm_g, w_kv_up, w_fourier, w_out,
           ln1_g, ln1_b, w_router, router_bias, w_gate, w_up, w_down, ws_gate, ws_up, ws_down,
           ln2_g, ln2_b):
    p = dict(w_in=w_in, q_norm_g=q_norm_g, w_q_up=w_q_up, kv_norm_g=kv_norm_g, w_kv_up=w_kv_up,
             w_fourier=w_fourier, w_out=w_out, ln1_g=ln1_g, ln1_b=ln1_b, w_gate=w_gate, w_up=w_up,
             w_down=w_down, ws_gate=ws_gate, ws_up=ws_up, ws_down=ws_down, ln2_g=ln2_g, ln2_b=ln2_b)
    layers = [_prep_layer(i, p) for i in range(w_in.shape[0])]
    rw = _prep_router(w_router, router_bias)
    return _trunks([x_prompt, x_sample], layers, rw)
```

```python
import functools
import math

import numpy as np
import jax
import jax.numpy as jnp
from jax import lax
from jax.experimental import pallas as pl
from jax.experimental.pallas import tpu as pltpu

F32 = jnp.float32
BF16 = jnp.bfloat16

D_MODEL = 1024
DEPTH = 2
FOURIER_WIDTH = 256
FOURIER_GROUPS = 4
FOURIER_GROUP_DIM = 64
N_HEADS = 6
QK_NOPE_DIM = 128
QK_ROPE_DIM = 64
V_HEAD_DIM = 128
Q_LORA_RANK = 384
KV_LORA_RANK = 256
ROPE_THETA = 10000.0
N_EXPERTS = 16
N_GROUPS = 4
EXPERTS_PER_GROUP = 4
D_FF = 512
DEEPNORM_ALPHA = (2 * DEPTH) ** 0.25
LN_EPS = 1e-5
RMS_EPS = 1e-6

LANES = 128
QK_PAD = 256
N_PAIRS = 6
N_CLASSES = N_GROUPS * N_PAIRS
PAIRS = ((0, 1), (0, 2), (0, 3), (1, 2), (1, 3), (2, 3))

TM_TOKEN = 512
TM_MOE = 256
META_ROWS = 8
TQ = 1024
TK = 1024
KV_UNROLL = 4
SUBLANES = 8
DFT_N1 = 64
DFT_COLS = 4096
DFT_KB = 8
ROW_CHUNK = 1024
VMEM_LIMIT = 48 * 1024 * 1024

SOFTMAX_SCALE = (QK_NOPE_DIM + QK_ROPE_DIM) ** -0.5
LOG2E = math.log2(math.e)


def _cparams(sem):
    return pltpu.CompilerParams(dimension_semantics=sem, vmem_limit_bytes=VMEM_LIMIT)


def _dot(a, b):
    return jnp.dot(a, b, preferred_element_type=F32)


def _tiled(n_tokens):
    return (n_tokens // SUBLANES, D_MODEL // LANES, SUBLANES, LANES)


def _load_rows(ref):
    if len(ref.shape) == 2:
        return ref[...]
    rows = ref.shape[0] * SUBLANES
    return jnp.concatenate([ref[:, c].reshape(rows, LANES) for c in range(ref.shape[1])], axis=1)


def _store_rows(ref, val):
    for c in range(ref.shape[1]):
        ref[:, c] = val[:, c * LANES:(c + 1) * LANES].reshape(ref.shape[0], SUBLANES, LANES)


def _batch_seq(x):
    return (x.shape[0], x.shape[1]) if x.ndim == 3 else (x.shape[0], x.shape[1] * SUBLANES)


def _token_tile(ref, j):
    return ref.at[lax.shift_right_logical(j, 3), :, jnp.bitwise_and(j, SUBLANES - 1), :]


def _inproj_body(x_ref, win_ref, qg_ref, kvg_ref, wqa_ref, wqb_ref, wkv_ref, cos_ref, sin_ref,
                 uf_ref, q_ref, k_ref, v_ref):
    x = _load_rows(x_ref)
    h = _dot(x.astype(BF16), win_ref[...])
    uf_ref[...] = h[:, :FOURIER_WIDTH].astype(BF16)

    def rms(c, g):
        return (c * lax.rsqrt(jnp.mean(c * c, axis=-1, keepdims=True) + RMS_EPS) * g).astype(BF16)

    c_q = rms(h[:, FOURIER_WIDTH:FOURIER_WIDTH + Q_LORA_RANK], qg_ref[...])
    kv_lo = FOURIER_WIDTH + Q_LORA_RANK
    c_kv = rms(h[:, kv_lo:kv_lo + KV_LORA_RANK], kvg_ref[...])
    cos = cos_ref[...]
    sin = sin_ref[...]
    t = h[:, kv_lo + KV_LORA_RANK:]
    k_rope = (t * cos + pltpu.roll(t, 64, 1) * sin).astype(BF16)
    qa = _dot(c_q, wqa_ref[...])
    qb = _dot(c_q, wqb_ref[...])
    kv = _dot(c_kv, wkv_ref[...])
    qs = SOFTMAX_SCALE * LOG2E
    for hd in range(N_HEADS):
        lo = hd * QK_PAD
        q_ref[hd, :, :LANES] = (qa[:, lo:lo + LANES] * qs).astype(BF16)
        q_rope = qa[:, lo + LANES:lo + QK_PAD] * cos + qb[:, hd * LANES:(hd + 1) * LANES] * sin
        q_ref[hd, :, LANES:] = (q_rope * qs).astype(BF16)
        k_ref[hd, :, :LANES] = kv[:, lo:lo + LANES].astype(BF16)
        k_ref[hd, :, LANES:] = k_rope
        v_ref[hd] = kv[:, lo + LANES:lo + QK_PAD].astype(BF16)


def _inproj(x, lw, cos_t, sin_t):
    B, S = _batch_seq(x)
    tm = min(TM_TOKEN, S)
    blk = (tm, D_MODEL) if x.ndim == 3 else _tiled(tm)
    x_spec = pl.BlockSpec((None,) + blk, lambda b, i: (b, i) + (0,) * (len(blk) - 1))
    full = lambda a: pl.BlockSpec(a.shape, lambda b, i: (0,) * a.ndim)
    weights = (lw["w_in"], lw["q_g"], lw["kv_g"], lw["wqa"], lw["wqb"], lw["wkv"])
    return pl.pallas_call(
        _inproj_body,
        grid=(B, S // tm),
        in_specs=[x_spec]
        + [full(w) for w in weights]
        + [pl.BlockSpec((tm, LANES), lambda b, i: (i, 0))] * 2,
        out_specs=[
            pl.BlockSpec((None, tm, FOURIER_WIDTH), lambda b, i: (b, i, 0)),
            pl.BlockSpec((None, N_HEADS, tm, QK_PAD), lambda b, i: (b, 0, i, 0)),
            pl.BlockSpec((None, N_HEADS, tm, QK_PAD), lambda b, i: (b, 0, i, 0)),
            pl.BlockSpec((None, N_HEADS, tm, V_HEAD_DIM), lambda b, i: (b, 0, i, 0)),
        ],
        out_shape=[
            jax.ShapeDtypeStruct((B, S, FOURIER_WIDTH), BF16),
            jax.ShapeDtypeStruct((B, N_HEADS, S, QK_PAD), BF16),
            jax.ShapeDtypeStruct((B, N_HEADS, S, QK_PAD), BF16),
            jax.ShapeDtypeStruct((B, N_HEADS, S, V_HEAD_DIM), BF16),
        ],
        compiler_params=_cparams(("parallel", "parallel")),
        name="inproj",
    )(x, *weights, cos_t, sin_t)


def _dft1_body(u_ref, w1_ref, tc_ref, ts_ref, z_ref):
    n1 = u_ref.shape[0]
    y = _dot(w1_ref[...], u_ref[...])
    yr, yi = y[:n1], y[n1:]
    tc, ts = tc_ref[...], ts_ref[...]
    z_ref[0] = (yr * tc + yi * ts).astype(BF16)
    z_ref[1] = (yi * tc - yr * ts).astype(BF16)


def _dft2_body(z_ref, w2_ref, cbd_ref, sbd_ref, wbd_ref, f_ref, *, norm):
    kb, n2, w = z_ref.shape[1:]
    z = jnp.concatenate([jnp.concatenate([z_ref[part, j] for j in range(kb)], axis=1)
                         for part in range(2)], axis=0)
    v = _dot(w2_ref[...], z)
    vr = jnp.concatenate([v[:n2, j * w:(j + 1) * w] for j in range(kb)], axis=0).astype(BF16)
    vi = jnp.concatenate([v[n2:, j * w:(j + 1) * w] for j in range(kb)], axis=0).astype(BF16)
    g = (_dot(vr, cbd_ref[...]) + _dot(vi, sbd_ref[...])) * norm
    f = _dot(g.astype(BF16), wbd_ref[...]).astype(BF16)
    for j in range(kb):
        f_ref[j] = f[j * n2:(j + 1) * n2]


def _dft_tables(S):
    n1, n2 = DFT_N1, S // DFT_N1

    def cs(n, rows, cols):
        ang = 2.0 * np.pi * ((np.outer(rows, cols)) % n) / n
        return np.cos(ang), np.sin(ang)

    c1, s1 = cs(n1, np.arange(n1), np.arange(n1))
    w1 = np.concatenate([c1, -s1], axis=0)
    tc, ts = cs(S, np.arange(n1), np.arange(n2))
    tc = np.repeat(tc, FOURIER_WIDTH, axis=1)
    ts = np.repeat(ts, FOURIER_WIDTH, axis=1)
    c2, s2 = cs(n2, np.arange(n2), np.arange(n2))
    w2 = np.block([[c2, s2], [-s2, c2]])
    cg, sg = cs(FOURIER_GROUP_DIM, np.arange(FOURIER_GROUP_DIM), np.arange(FOURIER_GROUP_DIM))
    eye = np.eye(FOURIER_GROUPS)
    return dict(
        w1=jnp.asarray(w1, BF16), tc=jnp.asarray(tc, F32), ts=jnp.asarray(ts, F32),
        w2=jnp.asarray(w2, BF16),
        cbd=jnp.asarray(np.kron(eye, cg), BF16), sbd=jnp.asarray(np.kron(eye, sg), BF16),
    )


def _fourier(u_f, wbd, tabs):
    B, S, W = u_f.shape
    n1, n2 = DFT_N1, S // DFT_N1
    cols = min(DFT_COLS, n2 * W)
    u2 = u_f.reshape(B, n1, n2 * W)
    z = pl.pallas_call(
        _dft1_body,
        grid=(B, n2 * W // cols),
        in_specs=[
            pl.BlockSpec((None, n1, cols), lambda b, c: (b, 0, c)),
            pl.BlockSpec((2 * n1, n1), lambda b, c: (0, 0)),
            pl.BlockSpec((n1, cols), lambda b, c: (0, c)),
            pl.BlockSpec((n1, cols), lambda b, c: (0, c)),
        ],
        out_specs=pl.BlockSpec((None, 2, n1, cols), lambda b, c: (b, 0, 0, c)),
        out_shape=jax.ShapeDtypeStruct((B, 2, n1, n2 * W), BF16),
        compiler_params=_cparams(("parallel", "parallel")),
        name="dft_stage1",
    )(u2, tabs["w1"], tabs["tc"], tabs["ts"])
    z5 = z.reshape(B, 2, n1, n2, W)
    kb = DFT_KB
    f4 = pl.pallas_call(
        functools.partial(_dft2_body, norm=float((S * FOURIER_GROUP_DIM) ** -0.5)),
        grid=(B, n1 // kb),
        in_specs=[
            pl.BlockSpec((None, 2, kb, n2, W), lambda b, k: (b, 0, k, 0, 0)),
            pl.BlockSpec((2 * n2, 2 * n2), lambda b, k: (0, 0)),
            pl.BlockSpec((W, W), lambda b, k: (0, 0)),
            pl.BlockSpec((W, W), lambda b, k: (0, 0)),
            pl.BlockSpec((W, W), lambda b, k: (0, 0)),
        ],
        out_specs=pl.BlockSpec((None, kb, n2, W), lambda b, k: (b, k, 0, 0)),
        out_shape=jax.ShapeDtypeStruct((B, n1, n2, W), BF16),
        compiler_params=_cparams(("parallel", "parallel")),
        name="dft_stage2",
    )(z5, tabs["w2"], tabs["cbd"], tabs["sbd"], wbd)
    return jnp.transpose(f4, (0, 2, 1, 3)).reshape(B, S, W)


def _attn_body(q_ref, k_ref, v_ref, o_ref, *, tk):
    q = q_ref[...]
    tq = q.shape[0]
    n_kv = k_ref.shape[0] // tk

    def step(j, carry):
        m, l, acc = carry
        off = pl.multiple_of(j * tk, tk)
        kj = k_ref[pl.ds(off, tk), :]
        vj = v_ref[pl.ds(off, tk), :]
        s = lax.dot_general(q, kj, (((1,), (1,)), ((), ())), preferred_element_type=F32)
        m_new = jnp.maximum(m, jnp.max(s, axis=-1, keepdims=True))
        alpha = jnp.exp2(m - m_new)
        p = jnp.exp2(s - m_new)
        l = alpha * l + jnp.sum(p, axis=-1, keepdims=True)
        acc = alpha * acc + _dot(p.astype(BF16), vj)
        return m_new, l, acc

    init = (jnp.full((tq, 1), -jnp.inf, F32), jnp.zeros((tq, 1), F32),
            jnp.zeros((tq, V_HEAD_DIM), F32))
    _, l, acc = lax.fori_loop(0, n_kv, step, init, unroll=math.gcd(KV_UNROLL, n_kv))
    o_ref[...] = (acc / l).astype(BF16)


def _attention(q, k, v):
    B, H, S, _ = q.shape
    tq, tk = min(TQ, S), min(TK, S)
    return pl.pallas_call(
        functools.partial(_attn_body, tk=tk),
        grid=(B, H, S // tq),
        in_specs=[
            pl.BlockSpec((None, None, tq, QK_PAD), lambda b, h, i: (b, h, i, 0)),
            pl.BlockSpec((None, None, S, QK_PAD), lambda b, h, i: (b, h, 0, 0)),
            pl.BlockSpec((None, None, S, V_HEAD_DIM), lambda b, h, i: (b, h, 0, 0)),
        ],
        out_specs=pl.BlockSpec((None, tq, V_HEAD_DIM), lambda b, h, i: (b, i, h)),
        out_shape=jax.ShapeDtypeStruct((B, S, H * V_HEAD_DIM), BF16),
        compiler_params=_cparams(("parallel", "parallel", "parallel")),
        name="attention",
    )(q, k, v)


def _layernorm(y, g, b):
    mu = jnp.mean(y, axis=-1, keepdims=True)
    d = y - mu
    var = jnp.mean(d * d, axis=-1, keepdims=True)
    return d * lax.rsqrt(var + LN_EPS) * g + b


def _sigmoid(x):
    return 1.0 / (1.0 + jnp.exp(-x))


def _outproj_body(x_ref, f_ref, a_ref, wf_ref, wa_ref, g_ref, b_ref, wra_ref, wrb_ref, bias_ref,
                  tril_ref, run0_ref, x1_ref, meta_ref, cnt_ref, run_ref):
    @pl.when(pl.program_id(0) == 0)
    def _():
        run_ref[...] = run0_ref[...]

    mix = _dot(f_ref[...], wf_ref[...]) + _dot(a_ref[...], wa_ref[...])
    x1 = _layernorm(DEEPNORM_ALPHA * _load_rows(x_ref) + mix, g_ref[...], b_ref[...])
    _store_rows(x1_ref, x1)
    xb = x1.astype(BF16)
    pair_sum = _sigmoid(_dot(xb, wra_ref[...])) + _sigmoid(_dot(xb, wrb_ref[...])) + bias_ref[...]
    lane = lax.broadcasted_iota(jnp.int32, pair_sum.shape, 1)
    best = jnp.max(pair_sum, axis=-1, keepdims=True)
    cls = jnp.min(jnp.where(pair_sum == best, lane, LANES), axis=-1, keepdims=True)
    onehot = lane == cls
    prefix = _dot(tril_ref[...], onehot.astype(BF16))
    run = run_ref[...]
    rank = jnp.sum(jnp.where(onehot, prefix + run, 0.0), axis=-1, keepdims=True) - 1.0
    meta = jnp.where(lane == 0, cls.astype(F32), jnp.where(lane == 1, rank, 0.0))
    meta_ref[...] = jnp.transpose(meta)[:META_ROWS].astype(jnp.int32)
    run = run + prefix[prefix.shape[0] - 1:, :]
    run_ref[...] = run
    cnt_ref[...] = run


def _outproj(x_tok, f2d, a2d, lw, rw, run0):
    T = f2d.shape[0]
    tm = min(TM_TOKEN, T)
    tril = jnp.asarray(np.tril(np.ones((tm, tm), np.float32)), BF16)
    full = lambda a: pl.BlockSpec(a.shape, lambda i: (0,) * a.ndim)
    row = lambda w: pl.BlockSpec((tm, w), lambda i: (i, 0))
    tok = pl.BlockSpec(_tiled(tm), lambda i: (i, 0, 0, 0))
    x_spec = row(D_MODEL) if x_tok.ndim == 2 else tok
    consts = (lw["w_out_f"], lw["w_out_a"], lw["ln1_g"], lw["ln1_b"], rw["wra"], rw["wrb"],
              rw["bias_ab"], tril, run0)
    return pl.pallas_call(
        _outproj_body,
        grid=(T // tm,),
        in_specs=[x_spec, row(FOURIER_WIDTH), row(N_HEADS * V_HEAD_DIM)] + [full(c) for c in consts],
        out_specs=[tok, pl.BlockSpec((META_ROWS, tm), lambda i: (0, i)),
                   pl.BlockSpec((1, LANES), lambda i: (0, 0))],
        out_shape=[
            jax.ShapeDtypeStruct(_tiled(T), F32),
            jax.ShapeDtypeStruct((META_ROWS, T), jnp.int32),
            jax.ShapeDtypeStruct((1, LANES), F32),
        ],
        scratch_shapes=[pltpu.VMEM((1, LANES), F32)],
        compiler_params=_cparams(("arbitrary",)),
        name="outproj_route",
    )(x_tok, f2d, a2d, *consts)


def _scatter_rows_body(pos_ref, x_ref, init_hbm, dst_hbm, sem, *, chunk):
    del init_hbm

    def copy(r, s):
        slot = pos_ref[r * SUBLANES + s]
        return pltpu.make_async_copy(x_ref.at[r, :, s, :], _token_tile(dst_hbm, slot), sem)

    _for_each_token(chunk, copy)


def _gather_rows_body(pos_ref, src_hbm, o_ref, sem, *, chunk):
    def copy(r, s):
        slot = pos_ref[r * SUBLANES + s]
        return pltpu.make_async_copy(_token_tile(src_hbm, slot), o_ref.at[r, :, s, :], sem)

    _for_each_token(chunk, copy)


def _for_each_token(chunk, copy):
    def issue(r, c):
        for s in range(SUBLANES):
            copy(r, s).start()
        return c

    def drain(r, c):
        for s in range(SUBLANES):
            copy(r, s).wait()
        return c

    lax.fori_loop(0, chunk // SUBLANES, issue, 0)
    lax.fori_loop(0, chunk // SUBLANES, drain, 0)


def _scatter_rows(src, pos, dst):
    T = pos.shape[0]
    chunk = min(ROW_CHUNK, T)
    return pl.pallas_call(
        functools.partial(_scatter_rows_body, chunk=chunk),
        grid=(T // chunk,),
        in_specs=[
            pl.BlockSpec((chunk,), lambda i: (i,), memory_space=pltpu.SMEM),
            pl.BlockSpec(_tiled(chunk), lambda i: (i, 0, 0, 0)),
            pl.BlockSpec(memory_space=pl.ANY),
        ],
        out_specs=pl.BlockSpec(memory_space=pl.ANY),
        out_shape=jax.ShapeDtypeStruct(dst.shape, dst.dtype),
        scratch_shapes=[pltpu.SemaphoreType.DMA(())],
        input_output_aliases={2: 0},
        compiler_params=_cparams(("arbitrary",)),
        name="scatter_rows",
    )(pos, src, dst)


def _gather_rows(src, pos):
    T = pos.shape[0]
    chunk = min(ROW_CHUNK, T)
    return pl.pallas_call(
        functools.partial(_gather_rows_body, chunk=chunk),
        grid=(T // chunk,),
        in_specs=[
            pl.BlockSpec((chunk,), lambda i: (i,), memory_space=pltpu.SMEM),
            pl.BlockSpec(memory_space=pl.ANY),
        ],
        out_specs=pl.BlockSpec(_tiled(chunk), lambda i: (i, 0, 0, 0)),
        out_shape=jax.ShapeDtypeStruct(_tiled(T), src.dtype),
        scratch_shapes=[pltpu.SemaphoreType.DMA(())],
        compiler_params=_cparams(("arbitrary",)),
        name="gather_rows",
    )(pos, src)


def _swiglu(xb, wg, wu, wd):
    g = _dot(xb, wg)
    h = (g * _sigmoid(g)) * _dot(xb, wu)
    return _dot(h.astype(BF16), wd)


def _moe_body(ea_ref, eb_ref, nused_ref, x_ref, wr_ref, wsg_ref, wsu_ref, wsd_ref,
              wga_ref, wua_ref, wda_ref, wgb_ref, wub_ref, wdb_ref, g_ref, b_ref, y_ref):
    i = pl.program_id(0)

    @pl.when(i < nused_ref[0])
    def _():
        x = _load_rows(x_ref)
        xb = x.astype(BF16)
        scores = _sigmoid(_dot(xb, wr_ref[...]))
        lane = lax.broadcasted_iota(jnp.int32, scores.shape, 1)
        sa = jnp.sum(jnp.where(lane == ea_ref[i], scores, 0.0), axis=-1, keepdims=True)
        sb = jnp.sum(jnp.where(lane == eb_ref[i], scores, 0.0), axis=-1, keepdims=True)
        inv = 1.0 / (sa + sb)
        y = _swiglu(xb, wsg_ref[...], wsu_ref[...], wsd_ref[...])
        y = y + (sa * inv) * _swiglu(xb, wga_ref[...], wua_ref[...], wda_ref[...])
        y = y + (sb * inv) * _swiglu(xb, wgb_ref[...], wub_ref[...], wdb_ref[...])
        _store_rows(y_ref, _layernorm(DEEPNORM_ALPHA * x + y, g_ref[...], b_ref[...]))

    @pl.when(i >= nused_ref[0])
    def _():
        y_ref[...] = jnp.zeros_like(y_ref)


def _moe(xs, ea, eb, nused, lw, rw):
    n_rows = xs.shape[0] * SUBLANES
    tm = TM_MOE
    full = lambda a: pl.BlockSpec(a.shape, lambda i, ea, eb, nu: (0,) * a.ndim)
    exp_a = lambda a: pl.BlockSpec((None,) + a.shape[1:], lambda i, ea, eb, nu: (ea[i], 0, 0))
    exp_b = lambda a: pl.BlockSpec((None,) + a.shape[1:], lambda i, ea, eb, nu: (eb[i], 0, 0))
    row = pl.BlockSpec(_tiled(tm), lambda i, ea, eb, nu: (i, 0, 0, 0))
    shared = (rw["wr_pad"], lw["ws_gate"], lw["ws_up"], lw["ws_down"])
    routed = (lw["w_gate"], lw["w_up"], lw["w_down"])
    grid_spec = pltpu.PrefetchScalarGridSpec(
        num_scalar_prefetch=3,
        grid=(n_rows // tm,),
        in_specs=[row] + [full(w) for w in shared] + [exp_a(w) for w in routed]
        + [exp_b(w) for w in routed] + [full(lw["ln2_g"]), full(lw["ln2_b"])],
        out_specs=row,
    )
    return pl.pallas_call(
        _moe_body,
        grid_spec=grid_spec,
        out_shape=jax.ShapeDtypeStruct(xs.shape, F32),
        compiler_params=_cparams(("arbitrary",)),
        name="moe_ln2",
    )(ea, eb, nused, xs, *shared, *routed, *routed, lw["ln2_g"], lw["ln2_b"])


def _rotate_half_cols(w):
    half = w.shape[1] // 2
    return jnp.concatenate([-w[:, half:], w[:, :half]], axis=1)


def _prep_layer(i, p):
    w_in = p["w_in"][i]
    k_r_lo = FOURIER_WIDTH + Q_LORA_RANK + KV_LORA_RANK
    w_in_ext = jnp.concatenate([w_in, _rotate_half_cols(w_in[:, k_r_lo:])], axis=1)
    wq = p["w_q_up"][i].reshape(Q_LORA_RANK, N_HEADS, QK_NOPE_DIM + QK_ROPE_DIM)
    zeros = jnp.zeros((Q_LORA_RANK, N_HEADS, QK_PAD - QK_NOPE_DIM - QK_ROPE_DIM), F32)
    wqa = jnp.concatenate([wq, zeros], axis=2).reshape(Q_LORA_RANK, N_HEADS * QK_PAD)
    rot = jnp.concatenate([-wq[:, :, QK_NOPE_DIM + QK_ROPE_DIM // 2:],
                           wq[:, :, QK_NOPE_DIM:QK_NOPE_DIM + QK_ROPE_DIM // 2], zeros], axis=2)
    wqb = rot.reshape(Q_LORA_RANK, N_HEADS * LANES)
    eye = jnp.eye(FOURIER_GROUPS, dtype=F32)
    wbd = (eye[:, None, :, None] * p["w_fourier"][i][:, :, None, :]).reshape(FOURIER_WIDTH, FOURIER_WIDTH)
    row = lambda v: v[i].reshape(1, -1).astype(F32)
    return dict(
        w_in=w_in_ext.astype(BF16), q_g=row(p["q_norm_g"]), kv_g=row(p["kv_norm_g"]),
        wqa=wqa.astype(BF16), wqb=wqb.astype(BF16), wkv=p["w_kv_up"][i].astype(BF16),
        wbd=wbd.astype(BF16),
        w_out_f=p["w_out"][i][:FOURIER_WIDTH].astype(BF16),
        w_out_a=p["w_out"][i][FOURIER_WIDTH:].astype(BF16),
        ln1_g=row(p["ln1_g"]), ln1_b=row(p["ln1_b"]), ln2_g=row(p["ln2_g"]), ln2_b=row(p["ln2_b"]),
        w_gate=p["w_gate"][i].astype(BF16), w_up=p["w_up"][i].astype(BF16),
        w_down=p["w_down"][i].astype(BF16),
        ws_gate=p["ws_gate"][i].astype(BF16), ws_up=p["ws_up"][i].astype(BF16),
        ws_down=p["ws_down"][i].astype(BF16),
    )


def _class_members():
    a = np.array([EXPERTS_PER_GROUP * g + PAIRS[q][0] for g in range(N_GROUPS) for q in range(N_PAIRS)])
    b = np.array([EXPERTS_PER_GROUP * g + PAIRS[q][1] for g in range(N_GROUPS) for q in range(N_PAIRS)])
    return a, b


def _prep_router(w_router, router_bias):
    a, b = _class_members()
    pad = jnp.zeros((D_MODEL, LANES - N_CLASSES), F32)
    wra = jnp.concatenate([w_router[:, a], pad], axis=1)
    wrb = jnp.concatenate([w_router[:, b], pad], axis=1)
    bias = router_bias.astype(F32)
    bias_ab = jnp.concatenate([bias[a] + bias[b], jnp.full((LANES - N_CLASSES,), -jnp.inf, F32)])
    wr_pad = jnp.concatenate([w_router, jnp.zeros((D_MODEL, LANES - N_EXPERTS), F32)], axis=1)
    return dict(wra=wra.astype(BF16), wrb=wrb.astype(BF16), bias_ab=bias_ab.reshape(1, LANES),
                wr_pad=wr_pad.astype(BF16))


def _rope_tables(S):
    pos = jnp.arange(S, dtype=F32)
    inv_freq = ROPE_THETA ** (-jnp.arange(0, QK_ROPE_DIM, 2, dtype=F32) / QK_ROPE_DIM)
    ang = pos[:, None] * inv_freq[None, :]
    zeros = jnp.zeros((S, LANES - QK_ROPE_DIM), F32)
    cos_t = jnp.concatenate([jnp.cos(ang), jnp.cos(ang), zeros], axis=1)
    sin_t = jnp.concatenate([jnp.sin(ang), jnp.sin(ang), zeros], axis=1)
    return cos_t, sin_t


def _dispatch_plan(metas, counts, n_tokens):
    tm = TM_MOE
    cnt = counts[0, :N_CLASSES].astype(jnp.int32)
    padded = ((cnt + tm - 1) // tm) * tm
    ends = jnp.cumsum(padded)
    starts = ends - padded
    pos = [(starts[m[0]] + m[1]).astype(jnp.int32) for m in metas]
    n_tiles = n_tokens // tm + N_CLASSES
    nused = ends[-1] // tm
    tile_row = jnp.minimum(jnp.arange(n_tiles, dtype=jnp.int32), nused - 1) * tm
    tile_cls = jnp.sum((ends[None, :] <= tile_row[:, None]).astype(jnp.int32), axis=1)
    tile_cls = jnp.minimum(tile_cls, N_CLASSES - 1)
    a, b = _class_members()
    ea = jnp.asarray(a, jnp.int32)[tile_cls]
    eb = jnp.asarray(b, jnp.int32)[tile_cls]
    return pos, ea, eb, nused.reshape(1).astype(jnp.int32), n_tiles * tm


def _mix(x, lw, rw, rope, tabs, run0):
    B, S = _batch_seq(x)
    T = B * S
    u_f, q, k, v = _inproj(x, lw, *rope)
    f = _fourier(u_f, lw["wbd"], tabs)
    a = _attention(q, k, v)
    x_tok = x.reshape((T, D_MODEL) if x.ndim == 3 else _tiled(T))
    return _outproj(x_tok, f.reshape(T, -1), a.reshape(T, -1), lw, rw, run0)


def _layer(xs, lw, rw, ropes, tabs):
    run = jnp.zeros((1, LANES), F32)
    x1s, metas = [], []
    for x, rope, tab in zip(xs, ropes, tabs):
        x1, meta, run = _mix(x, lw, rw, rope, tab, run)
        x1s.append(x1)
        metas.append(meta)
    n_tokens = sum(m.shape[1] for m in metas)
    pos, ea, eb, nused, n_rows = _dispatch_plan(metas, run, n_tokens)
    sorted_x = jnp.zeros(_tiled(n_rows), F32)
    for x1, p in zip(x1s, pos):
        sorted_x = _scatter_rows(x1, p, sorted_x)
    sorted_y = _moe(sorted_x, ea, eb, nused, lw, rw)
    out = []
    for x, p in zip(xs, pos):
        B, S = _batch_seq(x)
        out.append(_gather_rows(sorted_y, p).reshape((B,) + _tiled(S)))
    return out


def _untile(x):
    B, S = _batch_seq(x)
    return jnp.transpose(x, (0, 1, 3, 2, 4)).reshape(B, S, D_MODEL)


def _trunks(xs, layers, rw):
    ropes = [_rope_tables(x.shape[1]) for x in xs]
    tabs = [_dft_tables(x.shape[1]) for x in xs]
    for lw in layers:
        xs = _layer(xs, lw, rw, ropes, tabs)
    return tuple(_untile(x) for x in xs)


def kernel(x_prompt, x_sample, w_in, q_norm_g, w_q_up, kv_norm_g, w_kv_up, w_fourier, w_out,
           ln1_g, ln1_b, w_router, router_bias, w_gate, w_up, w_down, ws_gate, ws_up, ws_down,
           ln2_g, ln2_b):
    p = dict(w_in=w_in, q_norm_g=q_norm_g, w_q_up=w_q_up, kv_norm_g=kv_norm_g, w_kv_up=w_kv_up,
             w_fourier=w_fourier, w_out=w_out, ln1_g=ln1_g, ln1_b=ln1_b, w_gate=w_gate, w_up=w_up,
             w_down=w_down, ws_gate=ws_gate, ws_up=ws_up, ws_down=ws_down, ln2_g=ln2_g, ln2_b=ln2_b)
    layers = [_prep_layer(i, p) for i in range(w_in.shape[0])]
    rw = _prep_router(w_router, router_bias)
    return _trunks([x_prompt, x_sample], layers, rw)
```

```python
import functools
import math

import numpy as np
import jax
import jax.numpy as jnp
from jax import lax
from jax.experimental import pallas as pl
from jax.experimental.pallas import tpu as pltpu

F32 = jnp.float32
BF16 = jnp.bfloat16

D_MODEL = 1024
DEPTH = 2
FOURIER_WIDTH = 256
FOURIER_GROUPS = 4
FOURIER_GROUP_DIM = 64
N_HEADS = 6
QK_NOPE_DIM = 128
QK_ROPE_DIM = 64
V_HEAD_DIM = 128
Q_LORA_RANK = 384
KV_LORA_RANK = 256
ROPE_THETA = 10000.0
N_EXPERTS = 16
N_GROUPS = 4
EXPERTS_PER_GROUP = 4
D_FF = 512
DEEPNORM_ALPHA = (2 * DEPTH) ** 0.25
LN_EPS = 1e-5
RMS_EPS = 1e-6

LANES = 128
QK_PAD = 256
N_PAIRS = 6
N_CLASSES = N_GROUPS * N_PAIRS
PAIRS = ((0, 1), (0, 2), (0, 3), (1, 2), (1, 3), (2, 3))

TM_TOKEN = 512
TM_MOE = 256
META_ROWS = 8
TQ = 1024
TK = 1024
KV_UNROLL = 4
SUBLANES = 8
DFT_N1 = 64
DFT_COLS = 4096
DFT_KB = 8
ROW_CHUNK = 1024
VMEM_LIMIT = 48 * 1024 * 1024

SOFTMAX_SCALE = (QK_NOPE_DIM + QK_ROPE_DIM) ** -0.5
LOG2E = math.log2(math.e)


def _cparams(sem):
    return pltpu.CompilerParams(dimension_semantics=sem, vmem_limit_bytes=VMEM_LIMIT)


def _dot(a, b):
    return jnp.dot(a, b, preferred_element_type=F32)


def _tiled(n_tokens):
    return (n_tokens // SUBLANES, D_MODEL // LANES, SUBLANES, LANES)


def _load_rows(ref):
    if len(ref.shape) == 2:
        return ref[...]
    rows = ref.shape[0] * SUBLANES
    return jnp.concatenate([ref[:, c].reshape(rows, LANES) for c in range(ref.shape[1])], axis=1)


def _store_rows(ref, val):
    for c in range(ref.shape[1]):
        ref[:, c] = val[:, c * LANES:(c + 1) * LANES].reshape(ref.shape[0], SUBLANES, LANES)


def _batch_seq(x):
    return (x.shape[0], x.shape[1]) if x.ndim == 3 else (x.shape[0], x.shape[1] * SUBLANES)


def _token_tile(ref, j):
    return ref.at[lax.shift_right_logical(j, 3), :, jnp.bitwise_and(j, SUBLANES - 1), :]


def _inproj_body(x_ref, win_ref, qg_ref, kvg_ref, wqa_ref, wqb_ref, wkv_ref, cos_ref, sin_ref,
                 uf_ref, q_ref, k_ref, v_ref):
    x = _load_rows(x_ref)
    h = _dot(x.astype(BF16), win_ref[...])
    uf_ref[...] = h[:, :FOURIER_WIDTH].astype(BF16)

    def rms(c, g):
        return (c * lax.rsqrt(jnp.mean(c * c, axis=-1, keepdims=True) + RMS_EPS) * g).astype(BF16)

    c_q = rms(h[:, FOURIER_WIDTH:FOURIER_WIDTH + Q_LORA_RANK], qg_ref[...])
    kv_lo = FOURIER_WIDTH + Q_LORA_RANK
    c_kv = rms(h[:, kv_lo:kv_lo + KV_LORA_RANK], kvg_ref[...])
    cos = cos_ref[...]
    sin = sin_ref[...]
    t = h[:, kv_lo + KV_LORA_RANK:]
    k_rope = (t * cos + pltpu.roll(t, 64, 1) * sin).astype(BF16)
    qa = _dot(c_q, wqa_ref[...])
    qb = _dot(c_q, wqb_ref[...])
    kv = _dot(c_kv, wkv_ref[...])
    qs = SOFTMAX_SCALE * LOG2E
    for hd in range(N_HEADS):
        lo = hd * QK_PAD
        q_ref[hd, :, :LANES] = (qa[:, lo:lo + LANES] * qs).astype(BF16)
        q_rope = qa[:, lo + LANES:lo + QK_PAD] * cos + qb[:, hd * LANES:(hd + 1) * LANES] * sin
        q_ref[hd, :, LANES:] = (q_rope * qs).astype(BF16)
        k_ref[hd, :, :LANES] = kv[:, lo:lo + LANES].astype(BF16)
        k_ref[hd, :, LANES:] = k_rope
        v_ref[hd] = kv[:, lo + LANES:lo + QK_PAD].astype(BF16)


def _inproj(x, lw, cos_t, sin_t):
    B, S = _batch_seq(x)
    tm = min(TM_TOKEN, S)
    blk = (tm, D_MODEL) if x.ndim == 3 else _tiled(tm)
    x_spec = pl.BlockSpec((None,) + blk, lambda b, i: (b, i) + (0,) * (len(blk) - 1))
    full = lambda a: pl.BlockSpec(a.shape, lambda b, i: (0,) * a.ndim)
    weights = (lw["w_in"], lw["q_g"], lw["kv_g"], lw["wqa"], lw["wqb"], lw["wkv"])
    return pl.pallas_call(
        _inproj_body,
        grid=(B, S // tm),
        in_specs=[x_spec]
        + [full(w) for w in weights]
        + [pl.BlockSpec((tm, LANES), lambda b, i: (i, 0))] * 2,
        out_specs=[
            pl.BlockSpec((None, tm, FOURIER_WIDTH), lambda b, i: (b, i, 0)),
            pl.BlockSpec((None, N_HEADS, tm, QK_PAD), lambda b, i: (b, 0, i, 0)),
            pl.BlockSpec((None, N_HEADS, tm, QK_PAD), lambda b, i: (b, 0, i, 0)),
            pl.BlockSpec((None, N_HEADS, tm, V_HEAD_DIM), lambda b, i: (b, 0, i, 0)),
        ],
        out_shape=[
            jax.ShapeDtypeStruct((B, S, FOURIER_WIDTH), BF16),
            jax.ShapeDtypeStruct((B, N_HEADS, S, QK_PAD), BF16),
            jax.ShapeDtypeStruct((B, N_HEADS, S, QK_PAD), BF16),
            jax.ShapeDtypeStruct((B, N_HEADS, S, V_HEAD_DIM), BF16),
        ],
        compiler_params=_cparams(("parallel", "parallel")),
        name="inproj",
    )(x, *weights, cos_t, sin_t)


def _dft1_body(u_ref, w1_ref, tc_ref, ts_ref, z_ref):
    n1 = u_ref.shape[0]
    y = _dot(w1_ref[...], u_ref[...])
    yr, yi = y[:n1], y[n1:]
    tc, ts = tc_ref[...], ts_ref[...]
    z_ref[0] = (yr * tc + yi * ts).astype(BF16)
    z_ref[1] = (yi * tc - yr * ts).astype(BF16)


def _dft2_body(z_ref, w2_ref, cbd_ref, sbd_ref, wbd_ref, f_ref, *, norm):
    kb, n2, w = z_ref.shape[1:]
    z = jnp.concatenate([jnp.concatenate([z_ref[part, j] for j in range(kb)], axis=1)
                         for part in range(2)], axis=0)
    v = _dot(w2_ref[...], z)
    vr = jnp.concatenate([v[:n2, j * w:(j + 1) * w] for j in range(kb)], axis=0).astype(BF16)
    vi = jnp.concatenate([v[n2:, j * w:(j + 1) * w] for j in range(kb)], axis=0).astype(BF16)
    g = (_dot(vr, cbd_ref[...]) + _dot(vi, sbd_ref[...])) * norm
    f = _dot(g.astype(BF16), wbd_ref[...]).astype(BF16)
    for j in range(kb):
        f_ref[j] = f[j * n2:(j + 1) * n2]


def _dft_tables(S):
    n1, n2 = DFT_N1, S // DFT_N1

    def cs(n, rows, cols):
        ang = 2.0 * np.pi * ((np.outer(rows, cols)) % n) / n
        return np.cos(ang), np.sin(ang)

    c1, s1 = cs(n1, np.arange(n1), np.arange(n1))
    w1 = np.concatenate([c1, -s1], axis=0)
    tc, ts = cs(S, np.arange(n1), np.arange(n2))
    tc = np.repeat(tc, FOURIER_WIDTH, axis=1)
    ts = np.repeat(ts, FOURIER_WIDTH, axis=1)
    c2, s2 = cs(n2, np.arange(n2), np.arange(n2))
    w2 = np.block([[c2, s2], [-s2, c2]])
    cg, sg = cs(FOURIER_GROUP_DIM, np.arange(FOURIER_GROUP_DIM), np.arange(FOURIER_GROUP_DIM))
    eye = np.eye(FOURIER_GROUPS)
    return dict(
        w1=jnp.asarray(w1, BF16), tc=jnp.asarray(tc, F32), ts=jnp.asarray(ts, F32),
        w2=jnp.asarray(w2, BF16),
        cbd=jnp.asarray(np.kron(eye, cg), BF16), sbd=jnp.asarray(np.kron(eye, sg), BF16),
    )


def _fourier(u_f, wbd, tabs):
    B, S, W = u_f.shape
    n1, n2 = DFT_N1, S // DFT_N1
    cols = min(DFT_COLS, n2 * W)
    u2 = u_f.reshape(B, n1, n2 * W)
    z = pl.pallas_call(
        _dft1_body,
        grid=(B, n2 * W // cols),
        in_specs=[
            pl.BlockSpec((None, n1, cols), lambda b, c: (b, 0, c)),
            pl.BlockSpec((2 * n1, n1), lambda b, c: (0, 0)),
            pl.BlockSpec((n1, cols), lambda b, c: (0, c)),
            pl.BlockSpec((n1, cols), lambda b, c: (0, c)),
        ],
        out_specs=pl.BlockSpec((None, 2, n1, cols), lambda b, c: (b, 0, 0, c)),
        out_shape=jax.ShapeDtypeStruct((B, 2, n1, n2 * W), BF16),
        compiler_params=_cparams(("parallel", "parallel")),
        name="dft_stage1",
    )(u2, tabs["w1"], tabs["tc"], tabs["ts"])
    z5 = z.reshape(B, 2, n1, n2, W)
    kb = DFT_KB
    f4 = pl.pallas_call(
        functools.partial(_dft2_body, norm=float((S * FOURIER_GROUP_DIM) ** -0.5)),
        grid=(B, n1 // kb),
        in_specs=[
            pl.BlockSpec((None, 2, kb, n2, W), lambda b, k: (b, 0, k, 0, 0)),
            pl.BlockSpec((2 * n2, 2 * n2), lambda b, k: (0, 0)),
            pl.BlockSpec((W, W), lambda b, k: (0, 0)),
            pl.BlockSpec((W, W), lambda b, k: (0, 0)),
            pl.BlockSpec((W, W), lambda b, k: (0, 0)),
        ],
        out_specs=pl.BlockSpec((None, kb, n2, W), lambda b, k: (b, k, 0, 0)),
        out_shape=jax.ShapeDtypeStruct((B, n1, n2, W), BF16),
        compiler_params=_cparams(("parallel", "parallel")),
        name="dft_stage2",
    )(z5, tabs["w2"], tabs["cbd"], tabs["sbd"], wbd)
    return jnp.transpose(f4, (0, 2, 1, 3)).reshape(B, S, W)


def _attn_body(q_ref, k_ref, v_ref, o_ref, *, tk):
    q = q_ref[...]
    tq = q.shape[0]
    n_kv = k_ref.shape[0] // tk

    def step(j, carry):
        m, l, acc = carry
        off = pl.multiple_of(j * tk, tk)
        kj = k_ref[pl.ds(off, tk), :]
        vj = v_ref[pl.ds(off, tk), :]
        s = lax.dot_general(q, kj, (((1,), (1,)), ((), ())), preferred_element_type=F32)
        m_new = jnp.maximum(m, jnp.max(s, axis=-1, keepdims=True))
        alpha = jnp.exp2(m - m_new)
        p = jnp.exp2(s - m_new)
        l = alpha * l + jnp.sum(p, axis=-1, keepdims=True)
        acc = alpha * acc + _dot(p.astype(BF16), vj)
        return m_new, l, acc

    init = (jnp.full((tq, 1), -jnp.inf, F32), jnp.zeros((tq, 1), F32),
            jnp.zeros((tq, V_HEAD_DIM), F32))
    _, l, acc = lax.fori_loop(0, n_kv, step, init, unroll=math.gcd(KV_UNROLL, n_kv))
    o_ref[...] = (acc / l).astype(BF16)


def _attention(q, k, v):
    B, H, S, _ = q.shape
    tq, tk = min(TQ, S), min(TK, S)
    return pl.pallas_call(
        functools.partial(_attn_body, tk=tk),
        grid=(B, H, S // tq),
        in_specs=[
            pl.BlockSpec((None, None, tq, QK_PAD), lambda b, h, i: (b, h, i, 0)),
            pl.BlockSpec((None, None, S, QK_PAD), lambda b, h, i: (b, h, 0, 0)),
            pl.BlockSpec((None, None, S, V_HEAD_DIM), lambda b, h, i: (b, h, 0, 0)),
        ],
        out_specs=pl.BlockSpec((None, tq, V_HEAD_DIM), lambda b, h, i: (b, i, h)),
        out_shape=jax.ShapeDtypeStruct((B, S, H * V_HEAD_DIM), BF16),
        compiler_params=_cparams(("parallel", "parallel", "parallel")),
        name="attention",
    )(q, k, v)


def _layernorm(y, g, b):
    mu = jnp.mean(y, axis=-1, keepdims=True)
    d = y - mu
    var = jnp.mean(d * d, axis=-1, keepdims=True)
    return d * lax.rsqrt(var + LN_EPS) * g + b


def _sigmoid(x):
    return 1.0 / (1.0 + jnp.exp(-x))


def _outproj_body(x_ref, f_ref, a_ref, wf_ref, wa_ref, g_ref, b_ref, wra_ref, wrb_ref, bias_ref,
                  tril_ref, run0_ref, x1_ref, meta_ref, cnt_ref, run_ref):
    @pl.when(pl.program_id(0) == 0)
    def _():
        run_ref[...] = run0_ref[...]

    mix = _dot(f_ref[...], wf_ref[...]) + _dot(a_ref[...], wa_ref[...])
    x1 = _layernorm(DEEPNORM_ALPHA * _load_rows(x_ref) + mix, g_ref[...], b_ref[...])
    _store_rows(x1_ref, x1)
    xb = x1.astype(BF16)
    pair_sum = _sigmoid(_dot(xb, wra_ref[...])) + _sigmoid(_dot(xb, wrb_ref[...])) + bias_ref[...]
    lane = lax.broadcasted_iota(jnp.int32, pair_sum.shape, 1)
    best = jnp.max(pair_sum, axis=-1, keepdims=True)
    cls = jnp.min(jnp.where(pair_sum == best, lane, LANES), axis=-1, keepdims=True)
    onehot = lane == cls
    prefix = _dot(tril_ref[...], onehot.astype(BF16))
    run = run_ref[...]
    rank = jnp.sum(jnp.where(onehot, prefix + run, 0.0), axis=-1, keepdims=True) - 1.0
    meta = jnp.where(lane == 0, cls.astype(F32), jnp.where(lane == 1, rank, 0.0))
    meta_ref[...] = jnp.transpose(meta)[:META_ROWS].astype(jnp.int32)
    run = run + prefix[prefix.shape[0] - 1:, :]
    run_ref[...] = run
    cnt_ref[...] = run


def _outproj(x_tok, f2d, a2d, lw, rw, run0):
    T = f2d.shape[0]
    tm = min(TM_TOKEN, T)
    tril = jnp.asarray(np.tril(np.ones((tm, tm), np.float32)), BF16)
    full = lambda a: pl.BlockSpec(a.shape, lambda i: (0,) * a.ndim)
    row = lambda w: pl.BlockSpec((tm, w), lambda i: (i, 0))
    tok = pl.BlockSpec(_tiled(tm), lambda i: (i, 0, 0, 0))
    x_spec = row(D_MODEL) if x_tok.ndim == 2 else tok
    consts = (lw["w_out_f"], lw["w_out_a"], lw["ln1_g"], lw["ln1_b"], rw["wra"], rw["wrb"],
              rw["bias_ab"], tril, run0)
    return pl.pallas_call(
        _outproj_body,
        grid=(T // tm,),
        in_specs=[x_spec, row(FOURIER_WIDTH), row(N_HEADS * V_HEAD_DIM)] + [full(c) for c in consts],
        out_specs=[tok, pl.BlockSpec((META_ROWS, tm), lambda i: (0, i)),
                   pl.BlockSpec((1, LANES), lambda i: (0, 0))],
        out_shape=[
            jax.ShapeDtypeStruct(_tiled(T), F32),
            jax.ShapeDtypeStruct((META_ROWS, T), jnp.int32),
            jax.ShapeDtypeStruct((1, LANES), F32),
        ],
        scratch_shapes=[pltpu.VMEM((1, LANES), F32)],
        compiler_params=_cparams(("arbitrary",)),
        name="outproj_route",
    )(x_tok, f2d, a2d, *consts)


def _scatter_rows_body(pos_ref, x_ref, init_hbm, dst_hbm, sem, *, chunk):
    del init_hbm

    def copy(r, s):
        slot = pos_ref[r * SUBLANES + s]
        return pltpu.make_async_copy(x_ref.at[r, :, s, :], _token_tile(dst_hbm, slot), sem)

    _for_each_token(chunk, copy)


def _gather_rows_body(pos_ref, src_hbm, o_ref, sem, *, chunk):
    def copy(r, s):
        slot = pos_ref[r * SUBLANES + s]
        return pltpu.make_async_copy(_token_tile(src_hbm, slot), o_ref.at[r, :, s, :], sem)

    _for_each_token(chunk, copy)


def _for_each_token(chunk, copy):
    def issue(r, c):
        for s in range(SUBLANES):
            copy(r, s).start(priority=s % 2)
        return c

    def drain(r, c):
        for s in range(SUBLANES):
            copy(r, s).wait()
        return c

    lax.fori_loop(0, chunk // SUBLANES, issue, 0)
    lax.fori_loop(0, chunk // SUBLANES, drain, 0)


def _scatter_rows(src, pos, dst):
    T = pos.shape[0]
    chunk = min(ROW_CHUNK, T)
    return pl.pallas_call(
        functools.partial(_scatter_rows_body, chunk=chunk),
        grid=(T // chunk,),
        in_specs=[
            pl.BlockSpec((chunk,), lambda i: (i,), memory_space=pltpu.SMEM),
            pl.BlockSpec(_tiled(chunk), lambda i: (i, 0, 0, 0)),
            pl.BlockSpec(memory_space=pl.ANY),
        ],
        out_specs=pl.BlockSpec(memory_space=pl.ANY),
        out_shape=jax.ShapeDtypeStruct(dst.shape, dst.dtype),
        scratch_shapes=[pltpu.SemaphoreType.DMA(())],
        input_output_aliases={2: 0},
        compiler_params=_cparams(("arbitrary",)),
        name="scatter_rows",
    )(pos, src, dst)


def _gather_rows(src, pos):
    T = pos.shape[0]
    chunk = min(ROW_CHUNK, T)
    return pl.pallas_call(
        functools.partial(_gather_rows_body, chunk=chunk),
        grid=(T // chunk,),
        in_specs=[
            pl.BlockSpec((chunk,), lambda i: (i,), memory_space=pltpu.SMEM),
            pl.BlockSpec(memory_space=pl.ANY),
        ],
        out_specs=pl.BlockSpec(_tiled(chunk), lambda i: (i, 0, 0, 0)),
        out_shape=jax.ShapeDtypeStruct(_tiled(T), src.dtype),
        scratch_shapes=[pltpu.SemaphoreType.DMA(())],
        compiler_params=_cparams(("arbitrary",)),
        name="gather_rows",
    )(pos, src)


def _swiglu(xb, wg, wu, wd):
    g = _dot(xb, wg)
    h = (g * _sigmoid(g)) * _dot(xb, wu)
    return _dot(h.astype(BF16), wd)


def _moe_body(ea_ref, eb_ref, nused_ref, x_ref, wr_ref, wsg_ref, wsu_ref, wsd_ref,
              wga_ref, wua_ref, wda_ref, wgb_ref, wub_ref, wdb_ref, g_ref, b_ref, y_ref, pre_ref):
    i = pl.program_id(0)

    @pl.when(i == 0)
    def _():
        pre_ref[...] = jnp.zeros_like(pre_ref)

    @pl.when(i <= nused_ref[0])
    def _():
        done = _layernorm(pre_ref[...], g_ref[...], b_ref[...])
        x = _load_rows(x_ref)
        xb = x.astype(BF16)
        scores = _sigmoid(_dot(xb, wr_ref[...]))
        lane = lax.broadcasted_iota(jnp.int32, scores.shape, 1)
        sa = jnp.sum(jnp.where(lane == ea_ref[i], scores, 0.0), axis=-1, keepdims=True)
        sb = jnp.sum(jnp.where(lane == eb_ref[i], scores, 0.0), axis=-1, keepdims=True)
        inv = 1.0 / (sa + sb)
        y = _swiglu(xb, wsg_ref[...], wsu_ref[...], wsd_ref[...])
        y = y + (sa * inv) * _swiglu(xb, wga_ref[...], wua_ref[...], wda_ref[...])
        y = y + (sb * inv) * _swiglu(xb, wgb_ref[...], wub_ref[...], wdb_ref[...])
        _store_rows(y_ref, done)
        pre_ref[...] = DEEPNORM_ALPHA * x + y

    @pl.when(i > nused_ref[0])
    def _():
        y_ref[...] = jnp.zeros_like(y_ref)


def _moe(xs, ea, eb, nused, lw, rw):
    n_rows = xs.shape[0] * SUBLANES
    tm = TM_MOE
    n_tiles = n_rows // tm
    full = lambda a: pl.BlockSpec(a.shape, lambda i, ea, eb, nu: (0,) * a.ndim)
    exp_a = lambda a: pl.BlockSpec((None,) + a.shape[1:], lambda i, ea, eb, nu: (ea[i], 0, 0))
    exp_b = lambda a: pl.BlockSpec((None,) + a.shape[1:], lambda i, ea, eb, nu: (eb[i], 0, 0))
    row_in = pl.BlockSpec(_tiled(tm), lambda i, ea, eb, nu: (jnp.minimum(i, n_tiles - 1), 0, 0, 0))
    row_out = pl.BlockSpec(_tiled(tm), lambda i, ea, eb, nu: (jnp.maximum(i - 1, 0), 0, 0, 0))
    shared = (rw["wr_pad"], lw["ws_gate"], lw["ws_up"], lw["ws_down"])
    routed = (lw["w_gate"], lw["w_up"], lw["w_down"])
    grid_spec = pltpu.PrefetchScalarGridSpec(
        num_scalar_prefetch=3,
        grid=(n_tiles + 1,),
        in_specs=[row_in] + [full(w) for w in shared] + [exp_a(w) for w in routed]
        + [exp_b(w) for w in routed] + [full(lw["ln2_g"]), full(lw["ln2_b"])],
        out_specs=row_out,
        scratch_shapes=[pltpu.VMEM((tm, D_MODEL), F32)],
    )
    return pl.pallas_call(
        _moe_body,
        grid_spec=grid_spec,
        out_shape=jax.ShapeDtypeStruct(xs.shape, F32),
        compiler_params=_cparams(("arbitrary",)),
        name="moe_ln2",
    )(ea, eb, nused, xs, *shared, *routed, *routed, lw["ln2_g"], lw["ln2_b"])


def _rotate_half_cols(w):
    half = w.shape[1] // 2
    return jnp.concatenate([-w[:, half:], w[:, :half]], axis=1)


def _prep_layer(i, p):
    w_in = p["w_in"][i]
    k_r_lo = FOURIER_WIDTH + Q_LORA_RANK + KV_LORA_RANK
    w_in_ext = jnp.concatenate([w_in, _rotate_half_cols(w_in[:, k_r_lo:])], axis=1)
    wq = p["w_q_up"][i].reshape(Q_LORA_RANK, N_HEADS, QK_NOPE_DIM + QK_ROPE_DIM)
    zeros = jnp.zeros((Q_LORA_RANK, N_HEADS, QK_PAD - QK_NOPE_DIM - QK_ROPE_DIM), F32)
    wqa = jnp.concatenate([wq, zeros], axis=2).reshape(Q_LORA_RANK, N_HEADS * QK_PAD)
    rot = jnp.concatenate([-wq[:, :, QK_NOPE_DIM + QK_ROPE_DIM // 2:],
                           wq[:, :, QK_NOPE_DIM:QK_NOPE_DIM + QK_ROPE_DIM // 2], zeros], axis=2)
    wqb = rot.reshape(Q_LORA_RANK, N_HEADS * LANES)
    eye = jnp.eye(FOURIER_GROUPS, dtype=F32)
    wbd = (eye[:, None, :, None] * p["w_fourier"][i][:, :, None, :]).reshape(FOURIER_WIDTH, FOURIER_WIDTH)
    row = lambda v: v[i].reshape(1, -1).astype(F32)
    return dict(
        w_in=w_in_ext.astype(BF16), q_g=row(p["q_norm_g"]), kv_g=row(p["kv_norm_g"]),
        wqa=wqa.astype(BF16), wqb=wqb.astype(BF16), wkv=p["w_kv_up"][i].astype(BF16),
        wbd=wbd.astype(BF16),
        w_out_f=p["w_out"][i][:FOURIER_WIDTH].astype(BF16),
        w_out_a=p["w_out"][i][FOURIER_WIDTH:].astype(BF16),
        ln1_g=row(p["ln1_g"]), ln1_b=row(p["ln1_b"]), ln2_g=row(p["ln2_g"]), ln2_b=row(p["ln2_b"]),
        w_gate=p["w_gate"][i].astype(BF16), w_up=p["w_up"][i].astype(BF16),
        w_down=p["w_down"][i].astype(BF16),
        ws_gate=p["ws_gate"][i].astype(BF16), ws_up=p["ws_up"][i].astype(BF16),
        ws_down=p["ws_down"][i].astype(BF16),
    )


def _class_members():
    a = np.array([EXPERTS_PER_GROUP * g + PAIRS[q][0] for g in range(N_GROUPS) for q in range(N_PAIRS)])
    b = np.array([EXPERTS_PER_GROUP * g + PAIRS[q][1] for g in range(N_GROUPS) for q in range(N_PAIRS)])
    return a, b


def _prep_router(w_router, router_bias):
    a, b = _class_members()
    pad = jnp.zeros((D_MODEL, LANES - N_CLASSES), F32)
    wra = jnp.concatenate([w_router[:, a], pad], axis=1)
    wrb = jnp.concatenate([w_router[:, b], pad], axis=1)
    bias = router_bias.astype(F32)
    bias_ab = jnp.concatenate([bias[a] + bias[b], jnp.full((LANES - N_CLASSES,), -jnp.inf, F32)])
    wr_pad = jnp.concatenate([w_router, jnp.zeros((D_MODEL, LANES - N_EXPERTS), F32)], axis=1)
    return dict(wra=wra.astype(BF16), wrb=wrb.astype(BF16), bias_ab=bias_ab.reshape(1, LANES),
                wr_pad=wr_pad.astype(BF16))


def _rope_tables(S):
    pos = jnp.arange(S, dtype=F32)
    inv_freq = ROPE_THETA ** (-jnp.arange(0, QK_ROPE_DIM, 2, dtype=F32) / QK_ROPE_DIM)
    ang = pos[:, None] * inv_freq[None, :]
    zeros = jnp.zeros((S, LANES - QK_ROPE_DIM), F32)
    cos_t = jnp.concatenate([jnp.cos(ang), jnp.cos(ang), zeros], axis=1)
    sin_t = jnp.concatenate([jnp.sin(ang), jnp.sin(ang), zeros], axis=1)
    return cos_t, sin_t


def _dispatch_plan(metas, counts, n_tokens):
    tm = TM_MOE
    cnt = counts[0, :N_CLASSES].astype(jnp.int32)
    padded = ((cnt + tm - 1) // tm) * tm
    ends = jnp.cumsum(padded)
    starts = ends - padded
    pos = [(starts[m[0]] + m[1]).astype(jnp.int32) for m in metas]
    n_tiles = n_tokens // tm + N_CLASSES
    nused = ends[-1] // tm
    tile_row = jnp.minimum(jnp.arange(n_tiles + 1, dtype=jnp.int32), nused - 1) * tm
    tile_cls = jnp.sum((ends[None, :] <= tile_row[:, None]).astype(jnp.int32), axis=1)
    tile_cls = jnp.minimum(tile_cls, N_CLASSES - 1)
    a, b = _class_members()
    ea = jnp.asarray(a, jnp.int32)[tile_cls]
    eb = jnp.asarray(b, jnp.int32)[tile_cls]
    return pos, ea, eb, nused.reshape(1).astype(jnp.int32), n_tiles * tm


def _mix(x, lw, rw, rope, tabs, run0):
    B, S = _batch_seq(x)
    T = B * S
    u_f, q, k, v = _inproj(x, lw, *rope)
    f = _fourier(u_f, lw["wbd"], tabs)
    a = _attention(q, k, v)
    x_tok = x.reshape((T, D_MODEL) if x.ndim == 3 else _tiled(T))
    return _outproj(x_tok, f.reshape(T, -1), a.reshape(T, -1), lw, rw, run0)


def _sorted_rows(xs):
    return sum(math.prod(_batch_seq(x)) for x in xs) + N_CLASSES * TM_MOE


def _layer(xs, sorted_x, lw, rw, ropes, tabs):
    run = jnp.zeros((1, LANES), F32)
    x1s, metas = [], []
    for x, rope, tab in zip(xs, ropes, tabs):
        x1, meta, run = _mix(x, lw, rw, rope, tab, run)
        x1s.append(x1)
        metas.append(meta)
    n_tokens = sum(m.shape[1] for m in metas)
    pos, ea, eb, nused, n_rows = _dispatch_plan(metas, run, n_tokens)
    assert sorted_x.shape == _tiled(n_rows)
    for x1, p in zip(x1s, pos):
        sorted_x = _scatter_rows(x1, p, sorted_x)
    sorted_y = _moe(sorted_x, ea, eb, nused, lw, rw)
    out = []
    for x, p in zip(xs, pos):
        B, S = _batch_seq(x)
        out.append(_gather_rows(sorted_y, p).reshape((B,) + _tiled(S)))
    return out, sorted_x


def _untile(x):
    B, S = _batch_seq(x)
    return jnp.transpose(x, (0, 1, 3, 2, 4)).reshape(B, S, D_MODEL)


def _trunks(xs, layers, rw):
    ropes = [_rope_tables(x.shape[1]) for x in xs]
    tabs = [_dft_tables(x.shape[1]) for x in xs]
    sorted_x = jnp.zeros(_tiled(_sorted_rows(xs)), F32)
    for lw in layers:
        xs, sorted_x = _layer(xs, sorted_x, lw, rw, ropes, tabs)
    return tuple(_untile(x) for x in xs)


def kernel(x_prompt, x_sample, w_in, q_norm_g, w_q_up, kv_norm_g, w_kv_up, w_fourier, w_out,
           ln1_g, ln1_b, w_router, router_bias, w_gate, w_up, w_down, ws_gate, ws_up, ws_down,
           ln2_g, ln2_b):
    p = dict(w_in=w_in, q_norm_g=q_norm_g, w_q_up=w_q_up, kv_norm_g=kv_norm_g, w_kv_up=w_kv_up,
             w_fourier=w_fourier, w_out=w_out, ln1_g=ln1_g, ln1_b=ln1_b, w_gate=w_gate, w_up=w_up,
             w_down=w_down, ws_gate=ws_gate, ws_up=ws_up, ws_down=ws_down, ln2_g=ln2_g, ln2_b=ln2_b)
    layers = [_prep_layer(i, p) for i in range(w_in.shape[0])]
    rw = _prep_router(w_router, router_bias)
    return _trunks([x_prompt, x_sample], layers, rw)
```
